```python
import jax, jax.numpy as jnp
from jax import lax
import numpy as np

D_MODEL = 2048
BATCH = 1
SEQ = 16384
DEPTH = 4

MIX = D_MODEL
HALF = MIX // 2
HEAD_DIM = 128
N_HEADS = HALF // HEAD_DIM
POOL_WINDOWS = (2, 4, 8, 16)
POOL_GROUPS = len(POOL_WINDOWS)
POOL_GW = HALF // POOL_GROUPS
MOBA_BLOCK = 256
MOBA_TOPK = 3
MOBA_Q_CHUNK = 64
DSA_TOPK_MAX = 256
DSA_IDX_HEADS = 8
DSA_IDX_DIM = 64
DSA_Q_CHUNK = 128
CONV_WIDTH = 31
N_EVEN = (DEPTH + 1) // 2
N_ODD = DEPTH // 2
EVEN_WIDTHS = (HALF, HALF, HALF, HALF, MIX)
EVEN_IN = 4 * HALF + MIX
ODD_WIDTHS = (HALF, HALF, HALF, DSA_IDX_HEADS * DSA_IDX_DIM, DSA_IDX_HEADS, DSA_IDX_DIM, HALF, HALF, MIX)
ODD_IN = 5 * HALF + DSA_IDX_HEADS * DSA_IDX_DIM + DSA_IDX_HEADS + DSA_IDX_DIM + MIX
EPS = 1e-6
NEG_INF = -1e30

kernel_name = 'hybrid_pool_moba_dsa_conformer_trunk'


def _offsets(widths):
    out, s = [], 0
    for w in widths[:-1]:
        s += w
        out.append(s)
    return out


def _rms(x, g):
    x32 = x.astype(jnp.float32)
    y = x32 * lax.rsqrt(jnp.mean(x32 * x32, axis=-1, keepdims=True) + EPS)
    return (y * g.astype(jnp.float32)).astype(x.dtype)


def _alibi_slopes(n):
    return jnp.exp2(-8.0 * jnp.arange(1, n + 1, dtype=jnp.float32) / n)


def _pool_mixer(u, pool_w, pool_scale):
    S = u.shape[1]
    u32 = u.astype(jnp.float32)
    cs = jnp.cumsum(u32, axis=1)
    t1 = jnp.arange(1, S + 1, dtype=jnp.float32)[None, :, None]
    outs = []
    for g, w in enumerate(POOL_WINDOWS):
        sl = slice(g * POOL_GW, (g + 1) * POOL_GW)
        csg = cs[:, :, sl]
        prev = jnp.pad(csg, ((0, 0), (w, 0), (0, 0)))[:, :S]
        mean = (csg - prev) / jnp.minimum(t1, float(w))
        outs.append(mean - u32[:, :, sl])
    d = jnp.stack(outs, axis=2).astype(u.dtype)
    y = jnp.einsum('bsgc,gcd->bsgd', d, pool_w).reshape(u.shape)
    return y * pool_scale


def _moba_attention(q, k, v):
    B_, S, H, Dh = q.shape
    nb = -(-S // MOBA_BLOCK)
    s_pad = nb * MOBA_BLOCK
    pad = ((0, 0), (0, s_pad - S), (0, 0), (0, 0))
    kb = jnp.pad(k, pad).reshape(B_, nb, MOBA_BLOCK, H, Dh).transpose(0, 3, 1, 2, 4)
    vb = jnp.pad(v, pad).reshape(B_, nb, MOBA_BLOCK, H, Dh).transpose(0, 3, 1, 2, 4)
    kmean = jnp.mean(kb.astype(jnp.float32), axis=3)
    n_sel = max(1, min(MOBA_TOPK, nb - 1))
    slopes = _alibi_slopes(H)[None, :, None, None]
    scale = Dh ** -0.5
    bi = jnp.arange(B_)[:, None, None, None]
    hi = jnp.arange(H)[None, :, None, None]
    blk_ids = jnp.arange(nb)
    in_blk = jnp.arange(MOBA_BLOCK)
    rank = jnp.arange(n_sel)

    def chunk(ci):
        t0 = ci * MOBA_Q_CHUNK
        tq = t0 + jnp.arange(MOBA_Q_CHUNK)
        qc = lax.dynamic_slice_in_dim(q, t0, MOBA_Q_CHUNK, axis=1).transpose(0, 2, 1, 3)
        qc32 = qc.astype(jnp.float32)
        own = t0 // MOBA_BLOCK
        k_own = lax.dynamic_index_in_dim(kb, own, axis=2, keepdims=False)
        v_own = lax.dynamic_index_in_dim(vb, own, axis=2, keepdims=False)
        dist_own = (tq[:, None] - (own * MOBA_BLOCK + in_blk)[None, :]).astype(jnp.float32)
        s_own = jnp.einsum('bhqd,bhkd->bhqk', qc32, k_own.astype(jnp.float32)) * scale - slopes * dist_own
        s_own = jnp.where(dist_own >= 0, s_own, NEG_INF)
        gsc = jnp.einsum('bhqd,bhnd->bhqn', qc32, kmean)
        gsc = jnp.where(blk_ids < own, gsc, NEG_INF)
        _, idx = lax.top_k(gsc, n_sel)
        valid = rank < own
        k_sel = kb[bi, hi, idx]
        v_sel = vb[bi, hi, idx]
        pos_sel = idx[..., None] * MOBA_BLOCK + in_blk
        dist_sel = (tq[None, None, :, None, None] - pos_sel).astype(jnp.float32)
        s_sel = jnp.einsum('bhqd,bhqrkd->bhqrk', qc32, k_sel.astype(jnp.float32)) * scale - slopes[..., None] * dist_sel
        s_sel = jnp.where(valid[:, None], s_sel, NEG_INF).reshape(B_, H, MOBA_Q_CHUNK, n_sel * MOBA_BLOCK)
        p = jax.nn.softmax(jnp.concatenate([s_own, s_sel], axis=-1), axis=-1)
        p_own = p[..., :MOBA_BLOCK]
        p_sel = p[..., MOBA_BLOCK:].reshape(B_, H, MOBA_Q_CHUNK, n_sel, MOBA_BLOCK)
        o = (jnp.einsum('bhqk,bhkd->bhqd', p_own, v_own.astype(jnp.float32))
             + jnp.einsum('bhqrk,bhqrkd->bhqd', p_sel, v_sel.astype(jnp.float32)))
        return o.astype(q.dtype).transpose(0, 2, 1, 3)

    out = lax.map(chunk, jnp.arange(S // MOBA_Q_CHUNK))
    return out.transpose(1, 0, 2, 3, 4).reshape(B_, S, H * Dh)


def _dsa_attention(q, k, v, qi, wi, ki):
    B_, S, H, Dh = q.shape
    n_sel = min(DSA_TOPK_MAX, S // 4)
    slopes = _alibi_slopes(H)[None, :, None, None]
    scale = Dh ** -0.5
    idx_scale = DSA_IDX_DIM ** -0.5
    bi = jnp.arange(B_)[:, None, None]
    pos = jnp.arange(S)

    def chunk(ci):
        t0 = ci * DSA_Q_CHUNK
        tq = t0 + jnp.arange(DSA_Q_CHUNK)
        qic = lax.dynamic_slice_in_dim(qi, t0, DSA_Q_CHUNK, axis=1).astype(jnp.float32)
        wic = lax.dynamic_slice_in_dim(wi, t0, DSA_Q_CHUNK, axis=1).astype(jnp.float32)
        logits = jax.nn.relu(jnp.einsum('bqhd,bsd->bqhs', qic, ki.astype(jnp.float32)) * idx_scale)
        score = jnp.einsum('bqh,bqhs->bqs', wic, logits)
        score = jnp.where(pos[None, None, :] <= tq[None, :, None], score, NEG_INF)
        _, idx = lax.top_k(score, n_sel)
        valid = idx <= tq[None, :, None]
        k_sel = k[bi, idx]
        v_sel = v[bi, idx]
        qc = lax.dynamic_slice_in_dim(q, t0, DSA_Q_CHUNK, axis=1).astype(jnp.float32)
        s = jnp.einsum('bqhd,bqkhd->bhqk', qc, k_sel.astype(jnp.float32)) * scale
        dist = (tq[None, :, None] - idx).astype(jnp.float32)[:, None]
        s = jnp.where(valid[:, None], s - slopes * dist, NEG_INF)
        p = jax.nn.softmax(s, axis=-1)
        o = jnp.einsum('bhqk,bqkhd->bqhd', p, v_sel.astype(jnp.float32))
        return o.astype(q.dtype)

    out = lax.map(chunk, jnp.arange(S // DSA_Q_CHUNK))
    return out.transpose(1, 0, 2, 3, 4).reshape(B_, S, H * Dh)


def _conv_module(a, b, conv_w, conv_b, ln_g, ln_b):
    u = a * jax.nn.sigmoid(b)
    y = lax.conv_general_dilated(u, conv_w[:, None, :].astype(u.dtype), window_strides=(1,),
                                 padding=[(CONV_WIDTH - 1, 0)],
                                 dimension_numbers=('NWC', 'WIO', 'NWC'),
                                 feature_group_count=HALF)
    y32 = (y + conv_b).astype(jnp.float32)
    mu = jnp.mean(y32, axis=-1, keepdims=True)
    var = jnp.mean(jnp.square(y32 - mu), axis=-1, keepdims=True)
    y32 = (y32 - mu) * lax.rsqrt(var + EPS) * ln_g.astype(jnp.float32) + ln_b.astype(jnp.float32)
    return jax.nn.silu(y32).astype(a.dtype)


def setup_inputs(seed: int = 0) -> dict:
    key = jax.random.key(seed)
    ks = jax.random.split(key, 18)
    f32 = jnp.float32

    def nrm(k, shape, s):
        return jax.random.normal(k, shape, f32) * s

    return {
        'x': nrm(ks[0], (BATCH, SEQ, D_MODEL), 1.0),
        'c': nrm(ks[1], (BATCH, D_MODEL), 1.0),
        'norm_g': 1.0 + nrm(ks[2], (DEPTH, D_MODEL), 0.02),
        'ada_w': nrm(ks[3], (DEPTH, D_MODEL, 3 * D_MODEL), 0.5 * D_MODEL ** -0.5),
        'ada_b': nrm(ks[4], (DEPTH, 3 * D_MODEL), 0.02),
        'w_out': nrm(ks[5], (DEPTH, MIX, D_MODEL), MIX ** -0.5),
        'ev_w_in': nrm(ks[6], (N_EVEN, D_MODEL, EVEN_IN), D_MODEL ** -0.5),
        'pool_w': nrm(ks[7], (N_EVEN, POOL_GROUPS, POOL_GW, POOL_GW), POOL_GW ** -0.5),
        'pool_scale': 1.0 + nrm(ks[8], (N_EVEN, HALF), 0.1),
        'moba_qn': 1.0 + nrm(ks[9], (N_EVEN, HEAD_DIM), 0.02),
        'moba_kn': 1.0 + nrm(ks[10], (N_EVEN, HEAD_DIM), 0.02),
        'od_w_in': nrm(ks[11], (N_ODD, D_MODEL, ODD_IN), D_MODEL ** -0.5),
        'dsa_qn': 1.0 + nrm(ks[12], (N_ODD, HEAD_DIM), 0.02),
        'dsa_kn': 1.0 + nrm(ks[13], (N_ODD, HEAD_DIM), 0.02),
        'conv_w': nrm(ks[14], (N_ODD, CONV_WIDTH, HALF), CONV_WIDTH ** -0.5),
        'conv_b': nrm(ks[15], (N_ODD, HALF), 0.02),
        'conv_ln_g': 1.0 + nrm(ks[16], (N_ODD, HALF), 0.02),
        'conv_ln_b': nrm(ks[17], (N_ODD, HALF), 0.02),
    }


def reference(x, c, norm_g, ada_w, ada_b, w_out, ev_w_in, pool_w, pool_scale, moba_qn, moba_kn,
              od_w_in, dsa_qn, dsa_kn, conv_w, conv_b, conv_ln_g, conv_ln_b):
    B_, S, _ = x.shape
    c_act = jax.nn.silu(c)
    for l in range(DEPTH):
        mod = c_act @ ada_w[l] + ada_b[l]
        shift, scl, gate_res = jnp.split(mod[:, None, :], 3, axis=-1)
        h = _rms(x, norm_g[l]) * (1.0 + scl) + shift
        if l % 2 == 0:
            e = l // 2
            z = h @ ev_w_in[e]
            u_a, q, k, v, g = jnp.split(z, _offsets(EVEN_WIDTHS), axis=-1)
            q = _rms(q.reshape(B_, S, N_HEADS, HEAD_DIM), moba_qn[e])
            k = _rms(k.reshape(B_, S, N_HEADS, HEAD_DIM), moba_kn[e])
            v = v.reshape(B_, S, N_HEADS, HEAD_DIM)
            y = jnp.concatenate([_pool_mixer(u_a, pool_w[e], pool_scale[e]),
                                 _moba_attention(q, k, v)], axis=-1)
        else:
            o = l // 2
            z = h @ od_w_in[o]
            q, k, v, qi, wi, ki, ga, gb, g = jnp.split(z, _offsets(ODD_WIDTHS), axis=-1)
            q = _rms(q.reshape(B_, S, N_HEADS, HEAD_DIM), dsa_qn[o])
            k = _rms(k.reshape(B_, S, N_HEADS, HEAD_DIM), dsa_kn[o])
            v = v.reshape(B_, S, N_HEADS, HEAD_DIM)
            qi = qi.reshape(B_, S, DSA_IDX_HEADS, DSA_IDX_DIM)
            wi = wi * (DSA_IDX_HEADS ** -0.5)
            y = jnp.concatenate([_dsa_attention(q, k, v, qi, wi, ki),
                                 _conv_module(ga, gb, conv_w[o], conv_b[o], conv_ln_g[o], conv_ln_b[o])], axis=-1)
        x = x + gate_res * ((y * jax.nn.silu(g)) @ w_out[l])
    return x
```

```python
import functools
import math

import jax
import jax.numpy as jnp
from jax import lax
from jax.experimental import pallas as pl
from jax.experimental.pallas import tpu as pltpu

F32 = jnp.float32
BF16 = jnp.bfloat16
I32 = jnp.int32

D_MODEL = 2048
HALF = D_MODEL // 2
HEAD_DIM = 128
N_HEADS = HALF // HEAD_DIM
POOL_WINDOWS = (2, 4, 8, 16)
POOL_GW = HALF // len(POOL_WINDOWS)
POOL_HALO = 16
BLK = 256
MOBA_TOPK = 3
DSA_TOPK_MAX = 256
IDX_HEADS = 8
IDX_DIM = 64
CONV_WIDTH = 31
CONV_HALO = 32
EPS = 1e-6
NEG = -1e30
LOG2E = 1.4426950408889634
INT_MIN = -(2 ** 31)

EV_N = 4 * HALF + D_MODEL
OD_N = 8192
OD_QI = 7 * HALF
OD_KI = OD_QI + IDX_HEADS * IDX_DIM
OD_WI = OD_KI + 128

VMEM_LIMIT = 56 * 1024 * 1024


def _cparams(n_axes, vmem=VMEM_LIMIT):
    return pltpu.CompilerParams(dimension_semantics=("arbitrary",) * n_axes, vmem_limit_bytes=vmem)


def _sigmoid(x):
    return 1.0 / (1.0 + jnp.exp(-x))


def _mods_kernel(c_ref, w_ref, b_ref, o_ref):
    c = c_ref[...]
    ca = c * _sigmoid(c)
    o_ref[0] = jnp.dot(ca, w_ref[0], precision=lax.Precision.HIGHEST,
                       preferred_element_type=F32) + b_ref[0]


def _mods(c, ada_w, ada_b):
    depth, d, n = ada_w.shape
    tn = 768
    c8 = jnp.broadcast_to(c.astype(F32), (8, d))
    return pl.pallas_call(
        _mods_kernel,
        grid=(depth, n // tn),
        in_specs=[pl.BlockSpec((8, d), lambda l, j: (0, 0)),
                  pl.BlockSpec((1, d, tn), lambda l, j: (l, 0, j)),
                  pl.BlockSpec((1, 1, tn), lambda l, j: (l, 0, j))],
        out_specs=pl.BlockSpec((1, 8, tn), lambda l, j: (l, 0, j)),
        out_shape=jax.ShapeDtypeStruct((depth, 8, n), F32),
        compiler_params=_cparams(2),
        name="adaln_mods",
    )(c8, ada_w, ada_b.reshape(depth, 1, n))


def _proj_kernel(x_ref, g_ref, scl_ref, sh_ref, w_ref, gain_ref, o_ref, h_ref, *, n_norm_tiles, tn):
    j = pl.program_id(1)

    @pl.when(j == 0)
    def _():
        x = x_ref[...]
        y = x * lax.rsqrt(jnp.mean(x * x, axis=-1, keepdims=True) + EPS)
        h = (y * g_ref[...]) * (1.0 + scl_ref[...]) + sh_ref[...]
        h_ref[...] = h.astype(BF16)

    z = jnp.dot(h_ref[...], w_ref[...], preferred_element_type=F32)

    @pl.when(j < n_norm_tiles)
    def _():
        for c in range(tn // HEAD_DIM):
            cs = slice(c * HEAD_DIM, (c + 1) * HEAD_DIM)
            zc = z[:, cs]
            r = lax.rsqrt(jnp.mean(zc * zc, axis=-1, keepdims=True) + EPS)
            o_ref[:, cs] = (zc * r * gain_ref[:, cs]).astype(BF16)

    @pl.when(j >= n_norm_tiles)
    def _():
        o_ref[...] = z.astype(BF16)


def _proj(x2, g, scl, sh, w, gain, *, n_norm_cols):
    s, d = x2.shape
    n = w.shape[1]
    tm = min(1024, s)
    tn = 512
    return pl.pallas_call(
        functools.partial(_proj_kernel, n_norm_tiles=n_norm_cols // tn, tn=tn),
        grid=(s // tm, n // tn),
        in_specs=[pl.BlockSpec((tm, d), lambda i, j: (i, 0)),
                  pl.BlockSpec((1, d), lambda i, j: (0, 0)),
                  pl.BlockSpec((1, d), lambda i, j: (0, 0)),
                  pl.BlockSpec((1, d), lambda i, j: (0, 0)),
                  pl.BlockSpec((d, tn), lambda i, j: (0, j)),
                  pl.BlockSpec((1, tn), lambda i, j: (0, j))],
        out_specs=pl.BlockSpec((tm, tn), lambda i, j: (i, j)),
        out_shape=jax.ShapeDtypeStruct((s, n), BF16),
        scratch_shapes=[pltpu.VMEM((tm, d), BF16)],
        compiler_params=_cparams(2),
        name="norm_proj",
    )(x2, g, scl, sh, w, gain)


def _out_kernel(ya_ref, yb_ref, ga_ref, gb_ref, x_ref, gr_ref, w_ref, o_ref, a_ref):
    j = pl.program_id(1)

    @pl.when(j == 0)
    def _():
        for y_ref, g_ref, cs in ((ya_ref, ga_ref, slice(0, HALF)), (yb_ref, gb_ref, slice(HALF, 2 * HALF))):
            g = g_ref[...].astype(F32)
            a_ref[:, cs] = (y_ref[...].astype(F32) * (g * _sigmoid(g))).astype(BF16)

    acc = jnp.dot(a_ref[...], w_ref[...], preferred_element_type=F32)
    o_ref[...] = x_ref[...] + gr_ref[...] * acc


def _outproj(ya, yb, z, g_col_blk, x2, gate_res, w):
    s, d = x2.shape
    tm = min(1024, s)
    tn = 512
    return pl.pallas_call(
        _out_kernel,
        grid=(s // tm, d // tn),
        in_specs=[pl.BlockSpec((tm, HALF), lambda i, j: (i, 0)),
                  pl.BlockSpec((tm, HALF), lambda i, j: (i, 0)),
                  pl.BlockSpec((tm, HALF), lambda i, j: (i, g_col_blk)),
                  pl.BlockSpec((tm, HALF), lambda i, j: (i, g_col_blk + 1)),
                  pl.BlockSpec((tm, tn), lambda i, j: (i, j)),
                  pl.BlockSpec((1, tn), lambda i, j: (0, j)),
                  pl.BlockSpec((2 * HALF, tn), lambda i, j: (0, j))],
        out_specs=pl.BlockSpec((tm, tn), lambda i, j: (i, j)),
        out_shape=jax.ShapeDtypeStruct((s, d), F32),
        scratch_shapes=[pltpu.VMEM((tm, 2 * HALF), BF16)],
        compiler_params=_cparams(2),
        name="gated_outproj",
    )(ya, yb, z, z, x2, gate_res, w)


def _pool_kernel(u_ref, halo_ref, w_ref, ps_ref, o_ref, ext_ref, *, tm):
    i = pl.program_id(0)
    halo = halo_ref[...].astype(F32)
    ext_ref[0:POOL_HALO, :] = jnp.where(i > 0, halo, 0.0)
    ext_ref[POOL_HALO:POOL_HALO + tm, :] = u_ref[...].astype(F32)
    t1 = (i * tm + 1 + lax.broadcasted_iota(I32, (tm, 1), 0)).astype(F32)
    for g, win in enumerate(POOL_WINDOWS):
        cs = slice(g * POOL_GW, (g + 1) * POOL_GW)
        u = ext_ref[POOL_HALO:POOL_HALO + tm, cs]
        acc = u
        for j in range(1, win):
            acc = acc + ext_ref[POOL_HALO - j:POOL_HALO - j + tm, cs]
        dlt = acc / jnp.minimum(t1, float(win)) - u
        y = jnp.dot(dlt.astype(BF16), w_ref[g], preferred_element_type=F32)
        o_ref[:, cs] = (y * ps_ref[:, cs]).astype(BF16)


def _pool(z, u_col_blk, pool_w, pool_scale):
    s = z.shape[0]
    tm = min(512, s)
    hb = tm // POOL_HALO
    return pl.pallas_call(
        functools.partial(_pool_kernel, tm=tm),
        grid=(s // tm,),
        in_specs=[pl.BlockSpec((tm, HALF), lambda i: (i, u_col_blk)),
                  pl.BlockSpec((POOL_HALO, HALF),
                               lambda i: (jnp.maximum(i * hb - 1, 0), u_col_blk * (HALF // HALF))),
                  pl.BlockSpec((len(POOL_WINDOWS), POOL_GW, POOL_GW), lambda i: (0, 0, 0)),
                  pl.BlockSpec((1, HALF), lambda i: (0, 0))],
        out_specs=pl.BlockSpec((tm, HALF), lambda i: (i, 0)),
        out_shape=jax.ShapeDtypeStruct((s, HALF), BF16),
        scratch_shapes=[pltpu.VMEM((POOL_HALO + tm, HALF), F32)],
        compiler_params=_cparams(1),
        name="pool_mixer",
    )(z, z, pool_w, pool_scale)


def _conv_kernel(a_ref, b_ref, ah_ref, bh_ref, cw_ref, cb_ref, lg_ref, lb_ref, o_ref, ext_ref, *, tm):
    i = pl.program_id(0)

    def glu(a, b):
        return a.astype(F32) * _sigmoid(b.astype(F32))

    ext_ref[0:CONV_HALO, :] = jnp.where(i > 0, glu(ah_ref[...], bh_ref[...]), 0.0)
    ext_ref[CONV_HALO:CONV_HALO + tm, :] = glu(a_ref[...], b_ref[...])
    off = CONV_HALO - (CONV_WIDTH - 1)
    acc = cw_ref[0:1, :] * ext_ref[off:off + tm, :]
    for j in range(1, CONV_WIDTH):
        acc = acc + cw_ref[j:j + 1, :] * ext_ref[off + j:off + j + tm, :]
    y = acc + cb_ref[...]
    mu = jnp.mean(y, axis=-1, keepdims=True)
    yc = y - mu
    var = jnp.mean(yc * yc, axis=-1, keepdims=True)
    yn = yc * lax.rsqrt(var + EPS) * lg_ref[...] + lb_ref[...]
    o_ref[...] = (yn * _sigmoid(yn)).astype(BF16)


def _conv(z, a_col_blk, b_col_blk, conv_w, conv_b, ln_g, ln_b):
    s = z.shape[0]
    tm = min(256, s)
    hb = tm // CONV_HALO
    cwp = jnp.zeros((32, HALF), F32).at[:CONV_WIDTH].set(conv_w)
    return pl.pallas_call(
        functools.partial(_conv_kernel, tm=tm),
        grid=(s // tm,),
        in_specs=[pl.BlockSpec((tm, HALF), lambda i: (i, a_col_blk)),
                  pl.BlockSpec((tm, HALF), lambda i: (i, b_col_blk)),
                  pl.BlockSpec((CONV_HALO, HALF), lambda i: (jnp.maximum(i * hb - 1, 0), a_col_blk)),
                  pl.BlockSpec((CONV_HALO, HALF), lambda i: (jnp.maximum(i * hb - 1, 0), b_col_blk)),
                  pl.BlockSpec((32, HALF), lambda i: (0, 0)),
                  pl.BlockSpec((1, HALF), lambda i: (0, 0)),
                  pl.BlockSpec((1, HALF), lambda i: (0, 0)),
                  pl.BlockSpec((1, HALF), lambda i: (0, 0))],
        out_specs=pl.BlockSpec((tm, HALF), lambda i: (i, 0)),
        out_shape=jax.ShapeDtypeStruct((s, HALF), BF16),
        scratch_shapes=[pltpu.VMEM((CONV_HALO + tm, HALF), F32)],
        compiler_params=_cparams(1),
        name="conv_module",
    )(z, z, z, z, cwp, conv_b, ln_g, ln_b)


def _rel_bias(slope2):
    jj = lax.broadcasted_iota(I32, (BLK, BLK), 0)
    ii = lax.broadcasted_iota(I32, (BLK, BLK), 1)
    return (ii - jj).astype(F32) * (-slope2), jj <= ii


def _flash_step(m, l, acc, t, c2, v_t, flag=None):
    bmax = jnp.max(t, axis=0, keepdims=True) + c2
    if flag is None:
        m_new = jnp.maximum(m, bmax)
        sub = m_new - c2
    else:
        m_new = jnp.where(flag, jnp.maximum(m, bmax), m)
        sub = jnp.where(flag, m_new, -NEG) - c2
    p = jnp.exp2(t - sub)
    alpha = jnp.exp2(m - m_new)
    l_new = alpha * l + jnp.sum(p, axis=0, keepdims=True)
    acc_new = alpha * acc + jnp.dot(v_t, p.astype(BF16), preferred_element_type=F32)
    return m_new, l_new, acc_new


def _scores_t(k_blk, q):
    return lax.dot_general(k_blk, q, (((1,), (1,)), ((), ())), preferred_element_type=F32)


def _moba_kernel(slopes_ref, q_ref, k_ref, vt_ref, o_ref, km_ref, sel_ref, rel_ref, *, nb, n_sel):
    h = pl.program_id(0)
    qi = pl.program_id(1)
    nbp = sel_ref.shape[0]

    @pl.when(qi == 0)
    def _():
        km_ref[...] = jnp.zeros_like(km_ref)

        def mean_body(b, carry):
            blk = k_ref[pl.ds(pl.multiple_of(b * BLK, BLK), BLK), :].astype(F32)
            km_ref[pl.ds(b, 1), :] = jnp.sum(blk, axis=0, keepdims=True) * (1.0 / BLK)
            return carry

        lax.fori_loop(0, nb, mean_body, 0)

    slope2 = slopes_ref[h] * LOG2E
    rel, causal = _rel_bias(slope2)
    rel_ref[...] = rel
    q = q_ref[...]

    g_t = lax.dot_general(km_ref[...], q.astype(F32), (((1,), (1,)), ((), ())),
                          precision=lax.Precision.HIGHEST, preferred_element_type=F32)
    row = lax.broadcasted_iota(I32, (nbp, BLK), 0)
    past = row < qi
    cur = jnp.where(past, g_t, NEG)
    picked = jnp.zeros((nbp, BLK), jnp.bool_)
    for _ in range(n_sel):
        mx = jnp.max(cur, axis=0, keepdims=True)
        first = jnp.min(jnp.where(cur == mx, row, nbp), axis=0, keepdims=True)
        hit = row == first
        picked = picked | hit
        cur = jnp.where(hit, -3e38, cur)
    sel_ref[...] = jnp.where(picked & past, 1.0, 0.0)

    def body(kb, carry):
        m, l, acc = carry
        k_blk = k_ref[pl.ds(pl.multiple_of(kb * BLK, BLK), BLK), :]
        t = _scores_t(k_blk, q) + rel_ref[...]
        c2 = (qi - kb).astype(F32) * (-slope2 * BLK)
        flag = sel_ref[pl.ds(kb, 1), :] > 0.5
        return _flash_step(m, l, acc, t, c2, vt_ref[0, kb], flag)

    init = (jnp.full((1, BLK), NEG, F32), jnp.zeros((1, BLK), F32), jnp.zeros((HEAD_DIM, BLK), F32))
    m, l, acc = lax.fori_loop(0, qi, body, init)

    k_own = k_ref[pl.ds(pl.multiple_of(qi * BLK, BLK), BLK), :]
    t = jnp.where(causal, _scores_t(k_own, q) + rel_ref[...], NEG)
    m, l, acc = _flash_step(m, l, acc, t, 0.0, vt_ref[0, qi])
    o_ref[...] = jnp.transpose(acc / l).astype(BF16)


def _moba(z, k_col_blk0, v_t, slopes):
    s = z.shape[0]
    nb = s // BLK
    nbp = -(-nb // 8) * 8
    n_sel = max(1, min(MOBA_TOPK, nb - 1))
    grid_spec = pltpu.PrefetchScalarGridSpec(
        num_scalar_prefetch=1,
        grid=(N_HEADS, nb),
        in_specs=[pl.BlockSpec((BLK, HEAD_DIM), lambda h, i, sl: (i, h)),
                  pl.BlockSpec((s, HEAD_DIM), lambda h, i, sl: (0, k_col_blk0 + h)),
                  pl.BlockSpec((1, nb, HEAD_DIM, BLK), lambda h, i, sl: (h, 0, 0, 0))],
        out_specs=pl.BlockSpec((BLK, HEAD_DIM), lambda h, i, sl: (i, h)),
        scratch_shapes=[pltpu.VMEM((nbp, HEAD_DIM), F32),
                        pltpu.VMEM((nbp, BLK), F32),
                        pltpu.VMEM((BLK, BLK), F32)],
    )
    return pl.pallas_call(
        functools.partial(_moba_kernel, nb=nb, n_sel=n_sel),
        grid_spec=grid_spec,
        out_shape=jax.ShapeDtypeStruct((s, HALF), BF16),
        compiler_params=_cparams(2),
        name="moba_attention",
    )(slopes, z, z, v_t)


def _sortable(x):
    b = pltpu.bitcast(x, I32)
    return jnp.where(b < 0, b ^ 0x7FFFFFFF, b)


def _dsa_kernel(slopes_ref, qi_ref, wi_ref, ki_ref, q_ref, k_ref, vt_ref, o_ref,
                key_ref, thr_ref, qm_ref, rel_ref, *, n_sel):
    qt = pl.program_id(0)
    h = pl.program_id(1)
    jj = lax.broadcasted_iota(I32, (BLK, BLK), 0)
    ii = lax.broadcasted_iota(I32, (BLK, BLK), 1)
    causal = jj <= ii

    @pl.when(h == 0)
    def _():
        w_t = jnp.transpose(wi_ref[...].astype(F32))[0:IDX_HEADS, :]
        w_t = w_t * (IDX_HEADS ** -0.5 * IDX_DIM ** -0.5)
        lane = lax.broadcasted_iota(I32, (BLK, 128), 1)
        for hp in range(IDX_HEADS // 2):
            pair = qi_ref[:, hp * 128:(hp + 1) * 128]
            qm_ref[2 * hp] = jnp.where(lane < IDX_DIM, pair, jnp.zeros_like(pair))
            qm_ref[2 * hp + 1] = jnp.where(lane >= IDX_DIM, pair, jnp.zeros_like(pair))
        key_neg = _sortable(jnp.full((BLK, BLK), NEG, F32))

        def score_body(kb, carry):
            ki_blk = ki_ref[pl.ds(pl.multiple_of(kb * BLK, BLK), BLK), :]
            sc = jnp.zeros((BLK, BLK), F32)
            for hh in range(IDX_HEADS):
                lg = _scores_t(ki_blk, qm_ref[hh])
                sc = sc + w_t[hh:hh + 1, :] * jnp.maximum(lg, 0.0)
            keys = jnp.where((kb < qt) | causal, _sortable(sc), key_neg)
            key_ref[pl.ds(pl.multiple_of(kb * BLK, BLK), BLK), :] = keys
            return carry

        lax.fori_loop(0, qt + 1, score_body, 0)

        def count_ge(cand):
            def cbody(kb, cnt):
                blk = key_ref[pl.ds(pl.multiple_of(kb * BLK, BLK), BLK), :]
                return cnt + jnp.sum(jnp.where(blk >= cand, 1, 0), axis=0, keepdims=True)
            return lax.fori_loop(0, qt + 1, cbody, jnp.zeros((1, BLK), I32))

        def bit_body(b, tb):
            cand_b = tb | lax.shift_left(jnp.int32(1), 31 - b)
            cnt = count_ge(cand_b ^ INT_MIN)
            return jnp.where(cnt >= n_sel, cand_b, tb)

        tb = lax.fori_loop(0, 32, bit_body, jnp.zeros((1, BLK), I32))
        thr_ref[...] = jnp.broadcast_to(tb ^ INT_MIN, thr_ref.shape)

    slope2 = slopes_ref[h] * LOG2E
    rel_ref[...] = (ii - jj).astype(F32) * (-slope2)
    q = q_ref[...]
    thr = thr_ref[0:1, :]

    def body(kb, carry):
        m, l, acc = carry
        row0 = pl.multiple_of(kb * BLK, BLK)
        msk = key_ref[pl.ds(row0, BLK), :] >= thr
        msk = msk & ((kb < qt) | causal)
        t = jnp.where(msk, _scores_t(k_ref[pl.ds(row0, BLK), :], q) + rel_ref[...], NEG)
        c2 = (qt - kb).astype(F32) * (-slope2 * BLK)
        return _flash_step(m, l, acc, t, c2, vt_ref[0, kb])

    init = (jnp.full((1, BLK), NEG, F32), jnp.zeros((1, BLK), F32), jnp.zeros((HEAD_DIM, BLK), F32))
    m, l, acc = lax.fori_loop(0, qt + 1, body, init)
    o_ref[...] = jnp.transpose(acc / l).astype(BF16)


def _dsa(z, v_t, slopes):
    s = z.shape[0]
    nb = s // BLK
    n_sel = min(DSA_TOPK_MAX, s // 4)
    grid_spec = pltpu.PrefetchScalarGridSpec(
        num_scalar_prefetch=1,
        grid=(nb, N_HEADS),
        in_specs=[pl.BlockSpec((BLK, IDX_HEADS * IDX_DIM), lambda i, h, sl: (i, OD_QI // (IDX_HEADS * IDX_DIM))),
                  pl.BlockSpec((BLK, 128), lambda i, h, sl: (i, OD_WI // 128)),
                  pl.BlockSpec((s, 128), lambda i, h, sl: (0, OD_KI // 128)),
                  pl.BlockSpec((BLK, HEAD_DIM), lambda i, h, sl: (i, h)),
                  pl.BlockSpec((s, HEAD_DIM), lambda i, h, sl: (0, N_HEADS + h)),
                  pl.BlockSpec((1, nb, HEAD_DIM, BLK), lambda i, h, sl: (h, 0, 0, 0))],
        out_specs=pl.BlockSpec((BLK, HEAD_DIM), lambda i, h, sl: (i, h)),
        scratch_shapes=[pltpu.VMEM((s, BLK), I32),
                        pltpu.VMEM((8, BLK), I32),
                        pltpu.VMEM((IDX_HEADS, BLK, 128), BF16),
                        pltpu.VMEM((BLK, BLK), F32)],
    )
    return pl.pallas_call(
        functools.partial(_dsa_kernel, n_sel=n_sel),
        grid_spec=grid_spec,
        out_shape=jax.ShapeDtypeStruct((s, HALF), BF16),
        compiler_params=_cparams(2),
        name="dsa_attention",
    )(slopes, z, z, z, z, z, v_t)


def _v_transposed(z, col0):
    s = z.shape[0]
    v = z[:, col0:col0 + HALF].reshape(s // BLK, BLK, N_HEADS, HEAD_DIM)
    return jnp.transpose(v, (2, 0, 3, 1))


def _qk_gain(qn, kn, n):
    gq = jnp.tile(qn.astype(F32), N_HEADS) * (HEAD_DIM ** -0.5 * LOG2E)
    gk = jnp.tile(kn.astype(F32), N_HEADS)
    return jnp.concatenate([gq, gk, jnp.ones((n - 2 * HALF,), F32)])[None, :]


def kernel(x, c, norm_g, ada_w, ada_b, w_out, ev_w_in, pool_w, pool_scale, moba_qn, moba_kn,
           od_w_in, dsa_qn, dsa_kn, conv_w, conv_b, conv_ln_g, conv_ln_b):
    b, s, d = x.shape
    assert b == 1 and d == D_MODEL and s % BLK == 0
    depth = norm_g.shape[0]
    x2 = x[0].astype(F32)
    mods = _mods(c, ada_w, ada_b)[:, 0:1, :]
    slopes = jnp.exp2(-8.0 * jnp.arange(1, N_HEADS + 1, dtype=F32) / N_HEADS)

    for l in range(depth):
        shift, scl, gate_res = (mods[l, :, i * d:(i + 1) * d] for i in range(3))
        g = norm_g[l][None, :].astype(F32)
        if l % 2 == 0:
            e = l // 2
            w = ev_w_in[e]
            w = jnp.concatenate([w[:, HALF:4 * HALF], w[:, 0:HALF], w[:, 4 * HALF:]], axis=1).astype(BF16)
            z = _proj(x2, g, scl, shift, w, _qk_gain(moba_qn[e], moba_kn[e], EV_N), n_norm_cols=2 * HALF)
            ya = _pool(z, 3, pool_w[e].astype(BF16), pool_scale[e][None, :].astype(F32))
            yb = _moba(z, N_HEADS, _v_transposed(z, 2 * HALF), slopes)
            g_blk = 4
        else:
            o = l // 2
            w = od_w_in[o]
            c_qi = 3 * HALF
            c_wi = c_qi + IDX_HEADS * IDX_DIM
            c_ki = c_wi + IDX_HEADS
            c_ga = c_ki + IDX_DIM
            ki_w = w[:, c_ki:c_ga]
            w = jnp.concatenate([
                w[:, 0:3 * HALF], w[:, c_ga:],
                w[:, c_qi:c_wi], ki_w, ki_w,
                w[:, c_wi:c_ki], jnp.zeros((d, OD_N - OD_WI - IDX_HEADS), w.dtype)], axis=1).astype(BF16)
            z = _proj(x2, g, scl, shift, w, _qk_gain(dsa_qn[o], dsa_kn[o], OD_N), n_norm_cols=2 * HALF)
            ya = _dsa(z, _v_transposed(z, 2 * HALF), slopes)
            yb = _conv(z, 3, 4, conv_w[o].astype(F32), conv_b[o][None, :].astype(F32),
                       conv_ln_g[o][None, :].astype(F32), conv_ln_b[o][None, :].astype(F32))
            g_blk = 5
        x2 = _outproj(ya, yb, z, g_blk, x2, gate_res, w_out[l].astype(BF16))
    return x2[None].astype(x.dtype)
```

```python
import functools
import math

import jax
import jax.numpy as jnp
from jax import lax
from jax.experimental import pallas as pl
from jax.experimental.pallas import tpu as pltpu

F32 = jnp.float32
BF16 = jnp.bfloat16
I32 = jnp.int32
I16 = jnp.int16

D_MODEL = 2048
HALF = D_MODEL // 2
HEAD_DIM = 128
N_HEADS = HALF // HEAD_DIM
POOL_WINDOWS = (2, 4, 8, 16)
POOL_GW = HALF // len(POOL_WINDOWS)
POOL_HALO = 16
BLK = 256
CHUNK = 4
MOBA_TOPK = 3
DSA_TOPK_MAX = 256
IDX_HEADS = 8
IDX_DIM = 64
CONV_WIDTH = 31
CONV_HALO = 32
EPS = 1e-6
NEG = -1e30
LOG2E = 1.4426950408889634
INT_MIN = -(2 ** 31)
I16_MIN = -(2 ** 15)

EV_N = 4 * HALF + D_MODEL
OD_N = 8192
OD_QI = 7 * HALF
OD_KI = OD_QI + IDX_HEADS * IDX_DIM
OD_WI = OD_KI + 128

VMEM_LIMIT = 56 * 1024 * 1024


def _cparams(n_axes, vmem=VMEM_LIMIT):
    return pltpu.CompilerParams(dimension_semantics=("arbitrary",) * n_axes, vmem_limit_bytes=vmem)


def _sigmoid(x):
    return 1.0 / (1.0 + jnp.exp(-x))


def _mods_kernel(c_ref, w_ref, b_ref, o_ref):
    c = c_ref[...]
    ca = c * _sigmoid(c)
    o_ref[0] = jnp.dot(ca, w_ref[0], precision=lax.Precision.HIGHEST,
                       preferred_element_type=F32) + b_ref[0]


def _mods(c, ada_w, ada_b):
    depth, d, n = ada_w.shape
    tn = 768
    c8 = jnp.broadcast_to(c.astype(F32), (8, d))
    return pl.pallas_call(
        _mods_kernel,
        grid=(depth, n // tn),
        in_specs=[pl.BlockSpec((8, d), lambda l, j: (0, 0)),
                  pl.BlockSpec((1, d, tn), lambda l, j: (l, 0, j)),
                  pl.BlockSpec((1, 1, tn), lambda l, j: (l, 0, j))],
        out_specs=pl.BlockSpec((1, 8, tn), lambda l, j: (l, 0, j)),
        out_shape=jax.ShapeDtypeStruct((depth, 8, n), F32),
        compiler_params=_cparams(2),
        name="adaln_mods",
    )(c8, ada_w, ada_b.reshape(depth, 1, n))


def _proj_kernel(x_ref, g_ref, scl_ref, sh_ref, w_ref, gain_ref, o_ref, h_ref, *, n_norm_tiles, tn):
    j = pl.program_id(1)

    @pl.when(j == 0)
    def _():
        x = x_ref[...]
        y = x * lax.rsqrt(jnp.mean(x * x, axis=-1, keepdims=True) + EPS)
        h = (y * g_ref[...]) * (1.0 + scl_ref[...]) + sh_ref[...]
        h_ref[...] = h.astype(BF16)

    z = jnp.dot(h_ref[...], w_ref[...], preferred_element_type=F32)

    @pl.when(j < n_norm_tiles)
    def _():
        for c in range(tn // HEAD_DIM):
            cs = slice(c * HEAD_DIM, (c + 1) * HEAD_DIM)
            zc = z[:, cs]
            r = lax.rsqrt(jnp.mean(zc * zc, axis=-1, keepdims=True) + EPS)
            o_ref[:, cs] = (zc * r * gain_ref[:, cs]).astype(BF16)

    @pl.when(j >= n_norm_tiles)
    def _():
        o_ref[...] = z.astype(BF16)


def _proj(x2, g, scl, sh, w, gain, *, n_norm_cols):
    s, d = x2.shape
    n = w.shape[1]
    tm = min(1024, s)
    tn = 512
    return pl.pallas_call(
        functools.partial(_proj_kernel, n_norm_tiles=n_norm_cols // tn, tn=tn),
        grid=(s // tm, n // tn),
        in_specs=[pl.BlockSpec((tm, d), lambda i, j: (i, 0)),
                  pl.BlockSpec((1, d), lambda i, j: (0, 0)),
                  pl.BlockSpec((1, d), lambda i, j: (0, 0)),
                  pl.BlockSpec((1, d), lambda i, j: (0, 0)),
                  pl.BlockSpec((d, tn), lambda i, j: (0, j)),
                  pl.BlockSpec((1, tn), lambda i, j: (0, j))],
        out_specs=pl.BlockSpec((tm, tn), lambda i, j: (i, j)),
        out_shape=jax.ShapeDtypeStruct((s, n), BF16),
        scratch_shapes=[pltpu.VMEM((tm, d), BF16)],
        compiler_params=_cparams(2),
        name="norm_proj",
    )(x2, g, scl, sh, w, gain)


def _out_kernel(ya_ref, yb_ref, ga_ref, gb_ref, x_ref, gr_ref, w_ref, o_ref, a_ref):
    j = pl.program_id(1)

    @pl.when(j == 0)
    def _():
        for y_ref, g_ref, cs in ((ya_ref, ga_ref, slice(0, HALF)), (yb_ref, gb_ref, slice(HALF, 2 * HALF))):
            g = g_ref[...].astype(F32)
            a_ref[:, cs] = (y_ref[...].astype(F32) * (g * _sigmoid(g))).astype(BF16)

    acc = jnp.dot(a_ref[...], w_ref[...], preferred_element_type=F32)
    o_ref[...] = x_ref[...] + gr_ref[...] * acc


def _outproj(ya, yb, z, g_col_blk, x2, gate_res, w):
    s, d = x2.shape
    tm = min(1024, s)
    tn = 512
    return pl.pallas_call(
        _out_kernel,
        grid=(s // tm, d // tn),
        in_specs=[pl.BlockSpec((tm, HALF), lambda i, j: (i, 0)),
                  pl.BlockSpec((tm, HALF), lambda i, j: (i, 0)),
                  pl.BlockSpec((tm, HALF), lambda i, j: (i, g_col_blk)),
                  pl.BlockSpec((tm, HALF), lambda i, j: (i, g_col_blk + 1)),
                  pl.BlockSpec((tm, tn), lambda i, j: (i, j)),
                  pl.BlockSpec((1, tn), lambda i, j: (0, j)),
                  pl.BlockSpec((2 * HALF, tn), lambda i, j: (0, j))],
        out_specs=pl.BlockSpec((tm, tn), lambda i, j: (i, j)),
        out_shape=jax.ShapeDtypeStruct((s, d), F32),
        scratch_shapes=[pltpu.VMEM((tm, 2 * HALF), BF16)],
        compiler_params=_cparams(2),
        name="gated_outproj",
    )(ya, yb, z, z, x2, gate_res, w)


def _pool_kernel(u_ref, halo_ref, w_ref, ps_ref, o_ref, ext_ref, *, tm):
    i = pl.program_id(0)
    halo = halo_ref[...].astype(F32)
    ext_ref[0:POOL_HALO, :] = jnp.where(i > 0, halo, 0.0)
    ext_ref[POOL_HALO:POOL_HALO + tm, :] = u_ref[...].astype(F32)
    t1 = (i * tm + 1 + lax.broadcasted_iota(I32, (tm, 1), 0)).astype(F32)
    for g, win in enumerate(POOL_WINDOWS):
        cs = slice(g * POOL_GW, (g + 1) * POOL_GW)
        u = ext_ref[POOL_HALO:POOL_HALO + tm, cs]
        acc = u
        for j in range(1, win):
            acc = acc + ext_ref[POOL_HALO - j:POOL_HALO - j + tm, cs]
        dlt = acc / jnp.minimum(t1, float(win)) - u
        y = jnp.dot(dlt.astype(BF16), w_ref[g], preferred_element_type=F32)
        o_ref[:, cs] = (y * ps_ref[:, cs]).astype(BF16)


def _pool(z, u_col_blk, pool_w, pool_scale):
    s = z.shape[0]
    tm = min(512, s)
    hb = tm // POOL_HALO
    return pl.pallas_call(
        functools.partial(_pool_kernel, tm=tm),
        grid=(s // tm,),
        in_specs=[pl.BlockSpec((tm, HALF), lambda i: (i, u_col_blk)),
                  pl.BlockSpec((POOL_HALO, HALF),
                               lambda i: (jnp.maximum(i * hb - 1, 0), u_col_blk * (HALF // HALF))),
                  pl.BlockSpec((len(POOL_WINDOWS), POOL_GW, POOL_GW), lambda i: (0, 0, 0)),
                  pl.BlockSpec((1, HALF), lambda i: (0, 0))],
        out_specs=pl.BlockSpec((tm, HALF), lambda i: (i, 0)),
        out_shape=jax.ShapeDtypeStruct((s, HALF), BF16),
        scratch_shapes=[pltpu.VMEM((POOL_HALO + tm, HALF), F32)],
        compiler_params=_cparams(1),
        name="pool_mixer",
    )(z, z, pool_w, pool_scale)


def _conv_kernel(a_ref, b_ref, ah_ref, bh_ref, cw_ref, cb_ref, lg_ref, lb_ref, o_ref, ext_ref, *, tm):
    i = pl.program_id(0)

    def glu(a, b):
        return a.astype(F32) * _sigmoid(b.astype(F32))

    ext_ref[0:CONV_HALO, :] = jnp.where(i > 0, glu(ah_ref[...], bh_ref[...]), 0.0)
    ext_ref[CONV_HALO:CONV_HALO + tm, :] = glu(a_ref[...], b_ref[...])
    off = CONV_HALO - (CONV_WIDTH - 1)
    acc = cw_ref[0:1, :] * ext_ref[off:off + tm, :]
    for j in range(1, CONV_WIDTH):
        acc = acc + cw_ref[j:j + 1, :] * ext_ref[off + j:off + j + tm, :]
    y = acc + cb_ref[...]
    mu = jnp.mean(y, axis=-1, keepdims=True)
    yc = y - mu
    var = jnp.mean(yc * yc, axis=-1, keepdims=True)
    yn = yc * lax.rsqrt(var + EPS) * lg_ref[...] + lb_ref[...]
    o_ref[...] = (yn * _sigmoid(yn)).astype(BF16)


def _conv(z, a_col_blk, b_col_blk, conv_w, conv_b, ln_g, ln_b):
    s = z.shape[0]
    tm = min(256, s)
    hb = tm // CONV_HALO
    cwp = jnp.zeros((32, HALF), F32).at[:CONV_WIDTH].set(conv_w)
    return pl.pallas_call(
        functools.partial(_conv_kernel, tm=tm),
        grid=(s // tm,),
        in_specs=[pl.BlockSpec((tm, HALF), lambda i: (i, a_col_blk)),
                  pl.BlockSpec((tm, HALF), lambda i: (i, b_col_blk)),
                  pl.BlockSpec((CONV_HALO, HALF), lambda i: (jnp.maximum(i * hb - 1, 0), a_col_blk)),
                  pl.BlockSpec((CONV_HALO, HALF), lambda i: (jnp.maximum(i * hb - 1, 0), b_col_blk)),
                  pl.BlockSpec((32, HALF), lambda i: (0, 0)),
                  pl.BlockSpec((1, HALF), lambda i: (0, 0)),
                  pl.BlockSpec((1, HALF), lambda i: (0, 0)),
                  pl.BlockSpec((1, HALF), lambda i: (0, 0))],
        out_specs=pl.BlockSpec((tm, HALF), lambda i: (i, 0)),
        out_shape=jax.ShapeDtypeStruct((s, HALF), BF16),
        scratch_shapes=[pltpu.VMEM((CONV_HALO + tm, HALF), F32)],
        compiler_params=_cparams(1),
        name="conv_module",
    )(z, z, z, z, cwp, conv_b, ln_g, ln_b)


def _rel_bias(slope2):
    jj = lax.broadcasted_iota(I32, (BLK, BLK), 0)
    ii = lax.broadcasted_iota(I32, (BLK, BLK), 1)
    return (ii - jj).astype(F32) * (-slope2), jj <= ii


def _stage_scores(s_ref, tiles, c2s, flags):
    bmax = None
    for b, (t, c2, flag) in enumerate(zip(tiles, c2s, flags)):
        s_ref[b * BLK:(b + 1) * BLK, :] = t
        bm = jnp.max(t, axis=0, keepdims=True) + c2
        if flag is not None:
            bm = jnp.where(flag, bm, NEG)
        bmax = bm if bmax is None else jnp.maximum(bmax, bm)
    return bmax


def _stage_softmax(s_ref, bmax, m, l, acc, c2s, flags, v_ts):
    m_new = jnp.maximum(m, bmax)
    alpha = jnp.exp2(m - m_new)
    lsum = None
    pv = None
    for b, (c2, flag, v_t) in enumerate(zip(c2s, flags, v_ts)):
        sub = m_new - c2
        if flag is not None:
            sub = jnp.where(flag, sub, -NEG)
        p = jnp.exp2(s_ref[b * BLK:(b + 1) * BLK, :] - sub)
        ps = jnp.sum(p, axis=0, keepdims=True)
        d = jnp.dot(v_t, p.astype(BF16), preferred_element_type=F32)
        lsum = ps if lsum is None else lsum + ps
        pv = d if pv is None else pv + d
    return m_new, alpha * l + lsum, alpha * acc + pv


def _attend(n, stage_x, stage_y, s_a, s_b):
    init = (jnp.full((1, BLK), NEG, F32), jnp.zeros((1, BLK), F32), jnp.zeros((HEAD_DIM, BLK), F32))

    def pair(j, carry):
        m, l, acc, bm_a = carry
        c = 2 * j
        bm_b = stage_x(c + 1, s_b)
        m, l, acc = stage_y(c, s_a, bm_a, m, l, acc)
        bm_a = stage_x(c + 2, s_a)
        m, l, acc = stage_y(c + 1, s_b, bm_b, m, l, acc)
        return m, l, acc, bm_a

    m, l, acc, bm_a = lax.fori_loop(0, n // 2, pair, init + (stage_x(0, s_a),))
    m, l, acc = lax.cond(n % 2 == 1,
                         lambda: stage_y(n - 1, s_a, bm_a, m, l, acc),
                         lambda: (m, l, acc))
    return acc / l


def _scores_t(k_blk, q):
    return lax.dot_general(k_blk, q, (((1,), (1,)), ((), ())), preferred_element_type=F32)


def _moba_kernel(slopes_ref, q_ref, k_ref, vt_ref, o_ref, km_ref, sel_ref, rel_ref, sa_ref, sb_ref,
                 *, nb, n_sel):
    h = pl.program_id(0)
    qi = pl.program_id(1)
    nbp = sel_ref.shape[0]

    @pl.when(qi == 0)
    def _():
        km_ref[...] = jnp.zeros_like(km_ref)

        def mean_body(b, carry):
            blk = k_ref[pl.ds(pl.multiple_of(b * BLK, BLK), BLK), :].astype(F32)
            km_ref[pl.ds(b, 1), :] = jnp.sum(blk, axis=0, keepdims=True) * (1.0 / BLK)
            return carry

        lax.fori_loop(0, nb, mean_body, 0)

    slope2 = slopes_ref[h] * LOG2E
    rel, causal = _rel_bias(slope2)
    rel_ref[...] = rel
    q = q_ref[...]

    g_t = lax.dot_general(km_ref[...], q.astype(F32), (((1,), (1,)), ((), ())),
                          precision=lax.Precision.HIGHEST, preferred_element_type=F32)
    row = lax.broadcasted_iota(I32, (nbp, BLK), 0)
    past = row < qi
    cur = jnp.where(past, g_t, NEG)
    picked = jnp.zeros((nbp, BLK), jnp.bool_)
    for _ in range(n_sel):
        mx = jnp.max(cur, axis=0, keepdims=True)
        first = jnp.min(jnp.where(cur == mx, row, nbp), axis=0, keepdims=True)
        hit = row == first
        picked = picked | hit
        cur = jnp.where(hit, -3e38, cur)
    sel_ref[...] = jnp.where(picked & past, 1.0, 0.0)

    n_main = qi // CHUNK

    def chunk_meta(c):
        c = jnp.minimum(c, n_main)
        tail = c == n_main
        kbs, c2s, flags = [], [], []
        for b in range(CHUNK):
            raw = jnp.where(tail, qi - (CHUNK - 1) + b, c * CHUNK + b)
            kb = jnp.clip(raw, 0, nb - 1)
            enabled = jnp.logical_or(jnp.logical_not(tail), raw >= n_main * CHUNK)
            sel_row = jnp.where(enabled, sel_ref[pl.ds(kb, 1), :], 0.0)
            if b == CHUNK - 1:
                sel_row = jnp.where(tail, 1.0, sel_row)
            kbs.append(kb)
            c2s.append((qi - kb).astype(F32) * (-slope2 * BLK))
            flags.append(sel_row > 0.5)
        return tail, kbs, c2s, flags

    def stage_x(c, s_ref):
        tail, kbs, c2s, flags = chunk_meta(c)
        tiles = []
        for b, kb in enumerate(kbs):
            t = _scores_t(k_ref[pl.ds(pl.multiple_of(kb * BLK, BLK), BLK), :], q) + rel_ref[...]
            if b == CHUNK - 1:
                t = jnp.where(tail, jnp.where(causal, t, NEG), t)
            tiles.append(t)
        return _stage_scores(s_ref, tiles, c2s, flags)

    def stage_y(c, s_ref, bmax, m, l, acc):
        _, kbs, c2s, flags = chunk_meta(c)
        return _stage_softmax(s_ref, bmax, m, l, acc, c2s, flags, [vt_ref[0, kb] for kb in kbs])

    o_ref[...] = jnp.transpose(_attend(n_main + 1, stage_x, stage_y, sa_ref, sb_ref)).astype(BF16)


def _moba(z, k_col_blk0, v_t, slopes):
    s = z.shape[0]
    nb = s // BLK
    assert nb % CHUNK == 0
    nbp = -(-nb // 8) * 8
    n_sel = max(1, min(MOBA_TOPK, nb - 1))
    grid_spec = pltpu.PrefetchScalarGridSpec(
        num_scalar_prefetch=1,
        grid=(N_HEADS, nb),
        in_specs=[pl.BlockSpec((BLK, HEAD_DIM), lambda h, i, sl: (i, h)),
                  pl.BlockSpec((s, HEAD_DIM), lambda h, i, sl: (0, k_col_blk0 + h)),
                  pl.BlockSpec((1, nb, HEAD_DIM, BLK), lambda h, i, sl: (h, 0, 0, 0))],
        out_specs=pl.BlockSpec((BLK, HEAD_DIM), lambda h, i, sl: (i, h)),
        scratch_shapes=[pltpu.VMEM((nbp, HEAD_DIM), F32),
                        pltpu.VMEM((nbp, BLK), F32),
                        pltpu.VMEM((BLK, BLK), F32),
                        pltpu.VMEM((CHUNK * BLK, BLK), F32),
                        pltpu.VMEM((CHUNK * BLK, BLK), F32)],
    )
    return pl.pallas_call(
        functools.partial(_moba_kernel, nb=nb, n_sel=n_sel),
        grid_spec=grid_spec,
        out_shape=jax.ShapeDtypeStruct((s, HALF), BF16),
        compiler_params=_cparams(2),
        name="moba_attention",
    )(slopes, z, z, v_t)


def _sortable(x):
    b = pltpu.bitcast(x, I32)
    return jnp.where(b < 0, b ^ 0x7FFFFFFF, b)


def _dsa_kernel(slopes_ref, qi_ref, wi_ref, ki_ref, q_ref, k_ref, vt_ref, o_ref,
                key_ref, half_ref, thr_ref, qm_ref, rel_ref, sa_ref, sb_ref, *, n_sel):
    qt = pl.program_id(0)
    h = pl.program_id(1)
    jj = lax.broadcasted_iota(I32, (BLK, BLK), 0)
    ii = lax.broadcasted_iota(I32, (BLK, BLK), 1)
    causal = jj <= ii
    n_chunks = (qt + CHUNK) // CHUNK

    @pl.when(h == 0)
    def _():
        w_t = jnp.transpose(wi_ref[...].astype(F32))[0:IDX_HEADS, :]
        w_t = w_t * (IDX_HEADS ** -0.5 * IDX_DIM ** -0.5)
        lane = lax.broadcasted_iota(I32, (BLK, 128), 1)
        for hp in range(IDX_HEADS // 2):
            pair = qi_ref[:, hp * 128:(hp + 1) * 128]
            qm_ref[2 * hp] = jnp.where(lane < IDX_DIM, pair, jnp.zeros_like(pair))
            qm_ref[2 * hp + 1] = jnp.where(lane >= IDX_DIM, pair, jnp.zeros_like(pair))
        def score_body(kb, carry):
            ki_blk = ki_ref[pl.ds(pl.multiple_of(kb * BLK, BLK), BLK), :]
            sc = jnp.zeros((BLK, BLK), F32)
            for hh in range(IDX_HEADS):
                lg = _scores_t(ki_blk, qm_ref[hh])
                sc = sc + w_t[hh:hh + 1, :] * jnp.maximum(lg, 0.0)
            keys = jnp.where((kb < qt) | causal, _sortable(sc), INT_MIN)
            rows = pl.ds(pl.multiple_of(kb * BLK, BLK), BLK)
            key_ref[rows, :] = keys
            half_ref[rows, :] = lax.shift_right_arithmetic(keys, 16).astype(I16)
            return carry

        lax.fori_loop(0, qt + 1, score_body, 0)

        def fill_body(kb, carry):
            rows = pl.ds(pl.multiple_of(kb * BLK, BLK), BLK)
            key_ref[rows, :] = jnp.full((BLK, BLK), INT_MIN, I32)
            half_ref[rows, :] = jnp.full((BLK, BLK), I16_MIN, I16)
            return carry

        lax.fori_loop(qt + 1, n_chunks * CHUNK, fill_body, 0)

        def count_ge16(cand):
            def cbody(c, cnts):
                blk = half_ref[pl.ds(pl.multiple_of(c * (CHUNK * BLK), CHUNK * BLK), CHUNK * BLK), :]
                ind = jnp.where(blk >= cand, jnp.int16(1), jnp.int16(0))
                cnts = list(cnts)
                for r in range(CHUNK * BLK // 16):
                    cnts[r % len(cnts)] = cnts[r % len(cnts)] + ind[r * 16:(r + 1) * 16]
                return tuple(cnts)
            cnts = lax.fori_loop(0, n_chunks, cbody, tuple(jnp.zeros((16, BLK), I16) for _ in range(4)))
            tot = cnts[0].astype(I32) + cnts[1].astype(I32) + cnts[2].astype(I32) + cnts[3].astype(I32)
            return jnp.sum(tot, axis=0, keepdims=True)

        def bisect16(need):
            def bit_body(b, tb):
                cand_b = tb | lax.shift_left(jnp.int32(1), 15 - b)
                cnt = count_ge16((cand_b + I16_MIN).astype(I16))
                return jnp.where(cnt >= need, cand_b, tb)
            return lax.fori_loop(0, 16, bit_body, jnp.zeros((1, BLK), I32))

        t_hi = bisect16(n_sel) + I16_MIN
        above = count_ge16(jnp.minimum(t_hi + 1, -I16_MIN - 1).astype(I16))
        above = jnp.where(t_hi == -I16_MIN - 1, 0, above)

        def low_body(c, carry):
            rows = pl.ds(pl.multiple_of(c * (CHUNK * BLK), CHUNK * BLK), CHUNK * BLK)
            keys = key_ref[rows, :]
            low = (keys & 0xFFFF) + I16_MIN
            same = lax.shift_right_arithmetic(keys, 16) == t_hi
            half_ref[rows, :] = jnp.where(same, low, I16_MIN).astype(I16)
            return carry

        lax.fori_loop(0, n_chunks, low_body, 0)
        t_lo = bisect16(n_sel - above)
        thr = jnp.maximum(lax.shift_left(t_hi, 16) + t_lo, INT_MIN + 1)
        thr_ref[...] = jnp.broadcast_to(thr, thr_ref.shape)

    slope2 = slopes_ref[h] * LOG2E
    rel_ref[...] = (ii - jj).astype(F32) * (-slope2)
    q = q_ref[...]
    thr = thr_ref[0:1, :]

    no_flags = [None] * CHUNK

    def chunk_meta(c):
        kbs = [jnp.minimum(c, n_chunks - 1) * CHUNK + b for b in range(CHUNK)]
        return kbs, [jnp.maximum(qt - kb, 0).astype(F32) * (-slope2 * BLK) for kb in kbs]

    def stage_x(c, s_ref):
        kbs, c2s = chunk_meta(c)
        tiles = []
        for kb in kbs:
            row0 = pl.multiple_of(kb * BLK, BLK)
            msk = key_ref[pl.ds(row0, BLK), :] >= thr
            tiles.append(jnp.where(msk, _scores_t(k_ref[pl.ds(row0, BLK), :], q) + rel_ref[...], NEG))
        return _stage_scores(s_ref, tiles, c2s, no_flags)

    def stage_y(c, s_ref, bmax, m, l, acc):
        kbs, c2s = chunk_meta(c)
        return _stage_softmax(s_ref, bmax, m, l, acc, c2s, no_flags, [vt_ref[0, kb] for kb in kbs])

    o_ref[...] = jnp.transpose(_attend(n_chunks, stage_x, stage_y, sa_ref, sb_ref)).astype(BF16)


def _dsa(z, v_t, slopes):
    s = z.shape[0]
    nb = s // BLK
    assert nb % CHUNK == 0
    n_sel = min(DSA_TOPK_MAX, s // 4)
    grid_spec = pltpu.PrefetchScalarGridSpec(
        num_scalar_prefetch=1,
        grid=(nb, N_HEADS),
        in_specs=[pl.BlockSpec((BLK, IDX_HEADS * IDX_DIM), lambda i, h, sl: (i, OD_QI // (IDX_HEADS * IDX_DIM))),
                  pl.BlockSpec((BLK, 128), lambda i, h, sl: (i, OD_WI // 128)),
                  pl.BlockSpec((s, 128), lambda i, h, sl: (0, OD_KI // 128)),
                  pl.BlockSpec((BLK, HEAD_DIM), lambda i, h, sl: (i, h)),
                  pl.BlockSpec((s, HEAD_DIM), lambda i, h, sl: (0, N_HEADS + h)),
                  pl.BlockSpec((1, nb, HEAD_DIM, BLK), lambda i, h, sl: (h, 0, 0, 0))],
        out_specs=pl.BlockSpec((BLK, HEAD_DIM), lambda i, h, sl: (i, h)),
        scratch_shapes=[pltpu.VMEM((s, BLK), I32),
                        pltpu.VMEM((s, BLK), I16),
                        pltpu.VMEM((8, BLK), I32),
                        pltpu.VMEM((IDX_HEADS, BLK, 128), BF16),
                        pltpu.VMEM((BLK, BLK), F32),
                        pltpu.VMEM((CHUNK * BLK, BLK), F32),
                        pltpu.VMEM((CHUNK * BLK, BLK), F32)],
    )
    return pl.pallas_call(
        functools.partial(_dsa_kernel, n_sel=n_sel),
        grid_spec=grid_spec,
        out_shape=jax.ShapeDtypeStruct((s, HALF), BF16),
        compiler_params=_cparams(2),
        name="dsa_attention",
    )(slopes, z, z, z, z, z, v_t)


def _v_transposed(z, col0):
    s = z.shape[0]
    v = z[:, col0:col0 + HALF].reshape(s // BLK, BLK, N_HEADS, HEAD_DIM)
    return jnp.transpose(v, (2, 0, 3, 1))


def _qk_gain(qn, kn, n):
    gq = jnp.tile(qn.astype(F32), N_HEADS) * (HEAD_DIM ** -0.5 * LOG2E)
    gk = jnp.tile(kn.astype(F32), N_HEADS)
    return jnp.concatenate([gq, gk, jnp.ones((n - 2 * HALF,), F32)])[None, :]


def kernel(x, c, norm_g, ada_w, ada_b, w_out, ev_w_in, pool_w, pool_scale, moba_qn, moba_kn,
           od_w_in, dsa_qn, dsa_kn, conv_w, conv_b, conv_ln_g, conv_ln_b):
    b, s, d = x.shape
    assert b == 1 and d == D_MODEL and s % BLK == 0
    depth = norm_g.shape[0]
    x2 = x[0].astype(F32)
    mods = _mods(c, ada_w, ada_b)[:, 0:1, :]
    slopes = jnp.exp2(-8.0 * jnp.arange(1, N_HEADS + 1, dtype=F32) / N_HEADS)

    for l in range(depth):
        shift, scl, gate_res = (mods[l, :, i * d:(i + 1) * d] for i in range(3))
        g = norm_g[l][None, :].astype(F32)
        if l % 2 == 0:
            e = l // 2
            w = ev_w_in[e]
            w = jnp.concatenate([w[:, HALF:4 * HALF], w[:, 0:HALF], w[:, 4 * HALF:]], axis=1).astype(BF16)
            z = _proj(x2, g, scl, shift, w, _qk_gain(moba_qn[e], moba_kn[e], EV_N), n_norm_cols=2 * HALF)
            ya = _pool(z, 3, pool_w[e].astype(BF16), pool_scale[e][None, :].astype(F32))
            yb = _moba(z, N_HEADS, _v_transposed(z, 2 * HALF), slopes)
            g_blk = 4
        else:
            o = l // 2
            w = od_w_in[o]
            c_qi = 3 * HALF
            c_wi = c_qi + IDX_HEADS * IDX_DIM
            c_ki = c_wi + IDX_HEADS
            c_ga = c_ki + IDX_DIM
            ki_w = w[:, c_ki:c_ga]
            w = jnp.concatenate([
                w[:, 0:3 * HALF], w[:, c_ga:],
                w[:, c_qi:c_wi], ki_w, ki_w,
                w[:, c_wi:c_ki], jnp.zeros((d, OD_N - OD_WI - IDX_HEADS), w.dtype)], axis=1).astype(BF16)
            z = _proj(x2, g, scl, shift, w, _qk_gain(dsa_qn[o], dsa_kn[o], OD_N), n_norm_cols=2 * HALF)
            ya = _dsa(z, _v_transposed(z, 2 * HALF), slopes)
            yb = _conv(z, 3, 4, conv_w[o].astype(F32), conv_b[o][None, :].astype(F32),
                       conv_ln_g[o][None, :].astype(F32), conv_ln_b[o][None, :].astype(F32))
            g_blk = 5
        x2 = _outproj(ya, yb, z, g_blk, x2, gate_res, w_out[l].astype(BF16))
    return x2[None].astype(x.dtype)
```

```python
import functools

import jax
import numpy as np
import jax.numpy as jnp
from jax import lax
from jax.experimental import pallas as pl
from jax.experimental.pallas import tpu as pltpu

F32 = jnp.float32
BF16 = jnp.bfloat16
I32 = jnp.int32
I16 = jnp.int16

D_MODEL = 2048
HALF = D_MODEL // 2
HEAD_DIM = 128
N_HEADS = HALF // HEAD_DIM
POOL_WINDOWS = (2, 4, 8, 16)
POOL_GW = HALF // len(POOL_WINDOWS)
POOL_HALO = 16
BLK = 256
CHUNK = 4
MOBA_TOPK = 3
DSA_TOPK_MAX = 256
IDX_HEADS = 8
IDX_DIM = 64
CONV_WIDTH = 31
CONV_HALO = 32
SUBLANES = 8
EPS = 1e-6
NEG = -1e30
LOG2E = 1.4426950408889634
INT_MIN = -(2 ** 31)
I16_MIN = -(2 ** 15)
NEG_BITS = int(np.array(NEG, np.float32).view(np.int32))
V_ROWS = HEAD_DIM + 16

EV_N = 4 * HALF + D_MODEL
OD_N = 8192
OD_QI = 7 * HALF
OD_KI = OD_QI + IDX_HEADS * IDX_DIM
OD_WI = OD_KI + 128

VMEM_LIMIT = 56 * 1024 * 1024


def _cparams(n_axes, vmem=VMEM_LIMIT):
    return pltpu.CompilerParams(dimension_semantics=("arbitrary",) * n_axes, vmem_limit_bytes=vmem)


def _sigmoid(x):
    return 1.0 / (1.0 + jnp.exp(-x))


def _mods_kernel(c_ref, w_ref, b_ref, o_ref):
    c = c_ref[...]
    ca = c * _sigmoid(c)
    o_ref[0] = jnp.dot(ca, w_ref[0], precision=lax.Precision.HIGHEST,
                       preferred_element_type=F32) + b_ref[0]


def _mods(c, ada_w, ada_b):
    depth, d, n = ada_w.shape
    tn = 768
    c8 = jnp.broadcast_to(c.astype(F32), (8, d))
    return pl.pallas_call(
        _mods_kernel,
        grid=(depth, n // tn),
        in_specs=[pl.BlockSpec((8, d), lambda l, j: (0, 0)),
                  pl.BlockSpec((1, d, tn), lambda l, j: (l, 0, j)),
                  pl.BlockSpec((1, 1, tn), lambda l, j: (l, 0, j))],
        out_specs=pl.BlockSpec((1, 8, tn), lambda l, j: (l, 0, j)),
        out_shape=jax.ShapeDtypeStruct((depth, 8, n), F32),
        compiler_params=_cparams(2),
        name="adaln_mods",
    )(c8, ada_w, ada_b.reshape(depth, 1, n))


def _proj_kernel(x_ref, g_ref, scl_ref, sh_ref, w_ref, gain_ref, o_ref, h_ref, *, n_norm_tiles, tn):
    j = pl.program_id(1)

    @pl.when(j == 0)
    def _():
        x = x_ref[...]
        y = x * lax.rsqrt(jnp.mean(x * x, axis=-1, keepdims=True) + EPS)
        h = (y * g_ref[...]) * (1.0 + scl_ref[...]) + sh_ref[...]
        h_ref[...] = h.astype(BF16)

    z = jnp.dot(h_ref[...], w_ref[...], preferred_element_type=F32)

    @pl.when(j < n_norm_tiles)
    def _():
        for c in range(tn // HEAD_DIM):
            cs = slice(c * HEAD_DIM, (c + 1) * HEAD_DIM)
            zc = z[:, cs]
            r = lax.rsqrt(jnp.mean(zc * zc, axis=-1, keepdims=True) + EPS)
            o_ref[:, cs] = (zc * r * gain_ref[:, cs]).astype(BF16)

    @pl.when(j >= n_norm_tiles)
    def _():
        o_ref[...] = z.astype(BF16)


def _proj(x2, g, scl, sh, w, gain, *, n_norm_cols):
    s, d = x2.shape
    n = w.shape[1]
    tm = min(1024, s)
    tn = 512
    return pl.pallas_call(
        functools.partial(_proj_kernel, n_norm_tiles=n_norm_cols // tn, tn=tn),
        grid=(s // tm, n // tn),
        in_specs=[pl.BlockSpec((tm, d), lambda i, j: (i, 0)),
                  pl.BlockSpec((1, d), lambda i, j: (0, 0)),
                  pl.BlockSpec((1, d), lambda i, j: (0, 0)),
                  pl.BlockSpec((1, d), lambda i, j: (0, 0)),
                  pl.BlockSpec((d, tn), lambda i, j: (0, j)),
                  pl.BlockSpec((1, tn), lambda i, j: (0, j))],
        out_specs=pl.BlockSpec((tm, tn), lambda i, j: (i, j)),
        out_shape=jax.ShapeDtypeStruct((s, n), BF16),
        scratch_shapes=[pltpu.VMEM((tm, d), BF16)],
        compiler_params=_cparams(2),
        name="norm_proj",
    )(x2, g, scl, sh, w, gain)


def _out_kernel(ya_ref, yb_ref, ga_ref, gb_ref, x_ref, gr_ref, w_ref, o_ref, a_ref):
    j = pl.program_id(1)

    @pl.when(j == 0)
    def _():
        for y_ref, g_ref, cs in ((ya_ref, ga_ref, slice(0, HALF)), (yb_ref, gb_ref, slice(HALF, 2 * HALF))):
            g = g_ref[...].astype(F32)
            a_ref[:, cs] = (y_ref[...].astype(F32) * (g * _sigmoid(g))).astype(BF16)

    acc = jnp.dot(a_ref[...], w_ref[...], preferred_element_type=F32)
    o_ref[...] = x_ref[...] + gr_ref[...] * acc


def _outproj(ya, yb, z, g_col_blk, x2, gate_res, w):
    s, d = x2.shape
    tm = min(1024, s)
    tn = 512
    return pl.pallas_call(
        _out_kernel,
        grid=(s // tm, d // tn),
        in_specs=[pl.BlockSpec((tm, HALF), lambda i, j: (i, 0)),
                  pl.BlockSpec((tm, HALF), lambda i, j: (i, 0)),
                  pl.BlockSpec((tm, HALF), lambda i, j: (i, g_col_blk)),
                  pl.BlockSpec((tm, HALF), lambda i, j: (i, g_col_blk + 1)),
                  pl.BlockSpec((tm, tn), lambda i, j: (i, j)),
                  pl.BlockSpec((1, tn), lambda i, j: (0, j)),
                  pl.BlockSpec((2 * HALF, tn), lambda i, j: (0, j))],
        out_specs=pl.BlockSpec((tm, tn), lambda i, j: (i, j)),
        out_shape=jax.ShapeDtypeStruct((s, d), F32),
        scratch_shapes=[pltpu.VMEM((tm, 2 * HALF), BF16)],
        compiler_params=_cparams(2),
        name="gated_outproj",
    )(ya, yb, z, z, x2, gate_res, w)


def _pool_kernel(u_ref, halo_ref, w_ref, ps_ref, o_ref, ext_ref, *, tm):
    i = pl.program_id(0)
    halo = halo_ref[...].astype(F32)
    ext_ref[0:POOL_HALO, :] = jnp.where(i > 0, halo, 0.0)
    ext_ref[POOL_HALO:POOL_HALO + tm, :] = u_ref[...].astype(F32)
    t1 = (i * tm + 1 + lax.broadcasted_iota(I32, (tm, 1), 0)).astype(F32)
    for g, win in enumerate(POOL_WINDOWS):
        cs = slice(g * POOL_GW, (g + 1) * POOL_GW)
        u = ext_ref[POOL_HALO:POOL_HALO + tm, cs]
        acc = u
        for j in range(1, win):
            acc = acc + ext_ref[POOL_HALO - j:POOL_HALO - j + tm, cs]
        dlt = acc / jnp.minimum(t1, float(win)) - u
        y = jnp.dot(dlt.astype(BF16), w_ref[g], preferred_element_type=F32)
        o_ref[:, cs] = (y * ps_ref[:, cs]).astype(BF16)


def _pool(z, u_col_blk, pool_w, pool_scale):
    s = z.shape[0]
    tm = min(512, s)
    hb = tm // POOL_HALO
    return pl.pallas_call(
        functools.partial(_pool_kernel, tm=tm),
        grid=(s // tm,),
        in_specs=[pl.BlockSpec((tm, HALF), lambda i: (i, u_col_blk)),
                  pl.BlockSpec((POOL_HALO, HALF),
                               lambda i: (jnp.maximum(i * hb - 1, 0), u_col_blk * (HALF // HALF))),
                  pl.BlockSpec((len(POOL_WINDOWS), POOL_GW, POOL_GW), lambda i: (0, 0, 0)),
                  pl.BlockSpec((1, HALF), lambda i: (0, 0))],
        out_specs=pl.BlockSpec((tm, HALF), lambda i: (i, 0)),
        out_shape=jax.ShapeDtypeStruct((s, HALF), BF16),
        scratch_shapes=[pltpu.VMEM((POOL_HALO + tm, HALF), F32)],
        compiler_params=_cparams(1),
        name="pool_mixer",
    )(z, z, pool_w, pool_scale)


def _conv_kernel(a_ref, b_ref, ah_ref, bh_ref, cw_ref, cb_ref, lg_ref, lb_ref, o_ref,
                 ext_ref, sh_ref, y_ref, *, tm):
    i = pl.program_id(0)

    def glu(a, b):
        return a.astype(F32) * _sigmoid(b.astype(F32))

    ext_ref[0:CONV_HALO, :] = jnp.where(i > 0, glu(ah_ref[...], bh_ref[...]), 0.0)
    ext_ref[CONV_HALO:CONV_HALO + tm, :] = glu(a_ref[...], b_ref[...])
    span = CONV_HALO + tm - SUBLANES
    for r in range(1, SUBLANES):
        sh_ref[r - 1, 0:span, :] = ext_ref[r:r + span, :]
    off = CONV_HALO - (CONV_WIDTH - 1)
    for ct in range(HALF // 128):
        cs = slice(ct * 128, (ct + 1) * 128)
        acc = None
        for j in range(CONV_WIDTH):
            r, base = (off + j) % SUBLANES, (off + j) // SUBLANES * SUBLANES
            src = ext_ref[base:base + tm, cs] if r == 0 else sh_ref[r - 1, base:base + tm, cs]
            term = cw_ref[j:j + 1, cs] * src
            acc = term if acc is None else acc + term
        y_ref[:, cs] = acc + cb_ref[:, cs]
    y = y_ref[...]
    mu = jnp.mean(y, axis=-1, keepdims=True)
    yc = y - mu
    var = jnp.mean(yc * yc, axis=-1, keepdims=True)
    yn = yc * lax.rsqrt(var + EPS) * lg_ref[...] + lb_ref[...]
    o_ref[...] = (yn * _sigmoid(yn)).astype(BF16)


def _conv(z, a_col_blk, b_col_blk, conv_w, conv_b, ln_g, ln_b):
    s = z.shape[0]
    tm = min(256, s)
    hb = tm // CONV_HALO
    cwp = jnp.zeros((32, HALF), F32).at[:CONV_WIDTH].set(conv_w)
    return pl.pallas_call(
        functools.partial(_conv_kernel, tm=tm),
        grid=(s // tm,),
        in_specs=[pl.BlockSpec((tm, HALF), lambda i: (i, a_col_blk)),
                  pl.BlockSpec((tm, HALF), lambda i: (i, b_col_blk)),
                  pl.BlockSpec((CONV_HALO, HALF), lambda i: (jnp.maximum(i * hb - 1, 0), a_col_blk)),
                  pl.BlockSpec((CONV_HALO, HALF), lambda i: (jnp.maximum(i * hb - 1, 0), b_col_blk)),
                  pl.BlockSpec((32, HALF), lambda i: (0, 0)),
                  pl.BlockSpec((1, HALF), lambda i: (0, 0)),
                  pl.BlockSpec((1, HALF), lambda i: (0, 0)),
                  pl.BlockSpec((1, HALF), lambda i: (0, 0))],
        out_specs=pl.BlockSpec((tm, HALF), lambda i: (i, 0)),
        out_shape=jax.ShapeDtypeStruct((s, HALF), BF16),
        scratch_shapes=[pltpu.VMEM((CONV_HALO + tm, HALF), F32),
                        pltpu.VMEM((SUBLANES - 1, CONV_HALO + tm, HALF), F32),
                        pltpu.VMEM((tm, HALF), F32)],
        compiler_params=_cparams(1),
        name="conv_module",
    )(z, z, z, z, cwp, conv_b, ln_g, ln_b)


def _alibi_operands(slopes):
    s2 = slopes.astype(F32) * LOG2E
    pieces = []
    rest = s2
    for _ in range(3):
        part = rest.astype(BF16).astype(F32)
        pieces.append(part)
        rest = rest - part
    pieces = jnp.broadcast_to(jnp.stack(pieces, axis=-1)[:, None, :], (N_HEADS, BLK, 3))
    pos = jnp.broadcast_to(jnp.arange(BLK, dtype=F32)[None, :, None], (N_HEADS, BLK, 3))
    pad = jnp.zeros((N_HEADS, BLK, HEAD_DIM - 6), F32)
    e_k = jnp.concatenate([pos, pieces, pad], axis=-1).astype(BF16)
    e_q = jnp.concatenate([pieces, -pos, pad], axis=-1).astype(BF16)
    return e_k, e_q


def _causal_tile():
    jj = lax.broadcasted_iota(I32, (BLK, BLK), 0)
    ii = lax.broadcasted_iota(I32, (BLK, BLK), 1)
    return jj <= ii


def _attend(n, meta, score_tile, value_tile, slots):
    def step(chunk_of, count, m, acc, x=None, y=None, z=None):
        if x is not None:
            ix, s_x = x
            meta_x, c2_x, flag_x = meta(chunk_of(ix))
        if y is not None:
            iy, s_y, p_y, bm_y = y
            _, c2_y, flag_y = meta(chunk_of(iy))
            m_new = jnp.maximum(m, bm_y)
            alpha_y = jnp.exp2(m - m_new)
        if z is not None:
            iz, p_z, alpha_z = z
            meta_z, _, _ = meta(chunk_of(iz))
        bm_x = None
        pv = None
        for b in range(CHUNK):
            rows = slice(b * BLK, (b + 1) * BLK)
            if x is not None:
                t = score_tile(meta_x, b)
                s_x[rows, :] = t
                bm = jnp.max(t, axis=0, keepdims=True) + c2_x[b]
                if flag_x[b] is not None:
                    bm = jnp.where(flag_x[b], bm, NEG)
                bm_x = bm if bm_x is None else jnp.maximum(bm_x, bm)
            if y is not None:
                sub = m_new - c2_y[b]
                if flag_y[b] is not None:
                    sub = jnp.where(flag_y[b], sub, -NEG)
                p_y[rows, :] = jnp.exp2(s_y[rows, :] - sub).astype(BF16)
            if z is not None:
                d = jnp.dot(value_tile(meta_z, b), p_z[rows, :], preferred_element_type=F32)
                pv = d if pv is None else pv + d
        out = {}
        if x is not None:
            out["bm"] = jnp.where(ix < count, bm_x, NEG)
        if y is not None:
            m = m_new
            out["alpha"] = alpha_y
        if z is not None:
            acc = alpha_z * acc + pv
        return m, acc, out

    def run(states, chunk_ofs, count):
        streams = range(len(states))

        def pair(j, carry):
            i = 2 * j
            mid = []
            for k in streams:
                m, acc, bm_b, alpha_a = carry[k]
                s_a, s_b, p_a, p_b = slots[k]
                mid.append(step(chunk_ofs[k], count, m, acc,
                                x=(i + 2, s_a), y=(i + 1, s_b, p_b, bm_b), z=(i, p_a, alpha_a)))
            new = []
            for k in streams:
                m, acc, o1 = mid[k]
                s_a, s_b, p_a, p_b = slots[k]
                m, acc, o2 = step(chunk_ofs[k], count, m, acc,
                                  x=(i + 3, s_b), y=(i + 2, s_a, p_a, o1["bm"]), z=(i + 1, p_b, o1["alpha"]))
                new.append((m, acc, o2["bm"], o2["alpha"]))
            return tuple(new)

        first = [step(chunk_ofs[k], count, *states[k], x=(0, slots[k][0])) for k in streams]
        init = []
        for k in streams:
            m, acc, o0 = first[k]
            s_a, s_b, p_a, _ = slots[k]
            m, acc, o1 = step(chunk_ofs[k], count, m, acc, x=(1, s_b), y=(0, s_a, p_a, o0["bm"]))
            init.append((m, acc, o1["bm"], o1["alpha"]))
        carry = lax.fori_loop(0, count // 2, pair, tuple(init))

        def last():
            return tuple(step(chunk_ofs[k], count, carry[k][0], carry[k][1],
                              z=(count - 1, slots[k][2], carry[k][3]))[:2] for k in streams)

        return lax.cond(count % 2 == 1, last, lambda: tuple((c[0], c[1]) for c in carry))

    fresh = (jnp.full((1, BLK), NEG, F32), jnp.zeros((V_ROWS, BLK), F32))
    ((_, acc),) = run([fresh], [lambda i: i], n)
    return acc[0:HEAD_DIM] / acc[HEAD_DIM:HEAD_DIM + 1]


def _pipeline_slots():
    return [pltpu.VMEM((CHUNK * BLK, BLK), F32)] * 2 + [pltpu.VMEM((CHUNK * BLK, BLK), BF16)] * 2


def _scores_t(k_blk, q):
    return lax.dot_general(k_blk, q, (((1,), (1,)), ((), ())), preferred_element_type=F32)


def _moba_kernel(slopes_ref, q_ref, k_ref, vt_ref, ek_ref, eq_ref, o_ref, km_ref, sel_ref,
                 *slot_refs, nb, n_sel):
    h = pl.program_id(0)
    qi = pl.program_id(1)
    nbp = sel_ref.shape[0]

    @pl.when(qi == 0)
    def _():
        km_ref[...] = jnp.zeros_like(km_ref)

        def mean_body(b, carry):
            blk = k_ref[pl.ds(pl.multiple_of(b * BLK, BLK), BLK), :].astype(F32)
            km_ref[pl.ds(b, 1), :] = jnp.sum(blk, axis=0, keepdims=True) * (1.0 / BLK)
            return carry

        lax.fori_loop(0, nb, mean_body, 0)

    slope2 = slopes_ref[h] * LOG2E
    causal = _causal_tile()
    q = q_ref[...]
    q_ext = jnp.concatenate([q, eq_ref[0]], axis=1)
    e_k = ek_ref[0]

    g_t = lax.dot_general(km_ref[...], q.astype(F32), (((1,), (1,)), ((), ())),
                          precision=lax.Precision.HIGHEST, preferred_element_type=F32)
    row = lax.broadcasted_iota(I32, (nbp, BLK), 0)
    past = row < qi
    cur = jnp.where(past, g_t, NEG)
    picked = jnp.zeros((nbp, BLK), jnp.bool_)
    for _ in range(n_sel):
        mx = jnp.max(cur, axis=0, keepdims=True)
        first = jnp.min(jnp.where(cur == mx, row, nbp), axis=0, keepdims=True)
        hit = row == first
        picked = picked | hit
        cur = jnp.where(hit, -3e38, cur)
    sel_ref[...] = jnp.where(picked & past, 1.0, 0.0)

    n_main = qi // CHUNK

    def chunk_meta(c):
        c = jnp.minimum(c, n_main)
        tail = c == n_main
        kbs, c2s, flags = [], [], []
        for b in range(CHUNK):
            raw = jnp.where(tail, qi - (CHUNK - 1) + b, c * CHUNK + b)
            kb = jnp.clip(raw, 0, nb - 1)
            enabled = jnp.logical_or(jnp.logical_not(tail), raw >= n_main * CHUNK)
            sel_row = jnp.where(enabled, sel_ref[pl.ds(kb, 1), :], 0.0)
            if b == CHUNK - 1:
                sel_row = jnp.where(tail, 1.0, sel_row)
            kbs.append(kb)
            c2s.append((qi - kb).astype(F32) * (-slope2 * BLK))
            flags.append(sel_row > 0.5)
        return (tail, kbs), c2s, flags

    def score_tile(state, b):
        tail, kbs = state
        k_blk = k_ref[pl.ds(pl.multiple_of(kbs[b] * BLK, BLK), BLK), :]
        t = _scores_t(jnp.concatenate([k_blk, e_k], axis=1), q_ext)
        if b == CHUNK - 1:
            t = jnp.where(tail, jnp.where(causal, t, NEG), t)
        return t

    def value_tile(state, b):
        return vt_ref[0, state[1][b]]

    out_t = _attend(n_main + 1, chunk_meta, score_tile, value_tile, (slot_refs,))
    o_ref[...] = jnp.transpose(out_t).astype(BF16)


def _moba(z, k_col_blk0, v_t, slopes, e_k, e_q):
    s = z.shape[0]
    nb = s // BLK
    assert nb % CHUNK == 0
    nbp = -(-nb // 8) * 8
    n_sel = max(1, min(MOBA_TOPK, nb - 1))
    grid_spec = pltpu.PrefetchScalarGridSpec(
        num_scalar_prefetch=1,
        grid=(N_HEADS, nb),
        in_specs=[pl.BlockSpec((BLK, HEAD_DIM), lambda h, i, sl: (i, h)),
                  pl.BlockSpec((s, HEAD_DIM), lambda h, i, sl: (0, k_col_blk0 + h)),
                  pl.BlockSpec((1, nb, V_ROWS, BLK), lambda h, i, sl: (h, 0, 0, 0)),
                  pl.BlockSpec((1, BLK, HEAD_DIM), lambda h, i, sl: (h, 0, 0)),
                  pl.BlockSpec((1, BLK, HEAD_DIM), lambda h, i, sl: (h, 0, 0))],
        out_specs=pl.BlockSpec((BLK, HEAD_DIM), lambda h, i, sl: (i, h)),
        scratch_shapes=[pltpu.VMEM((nbp, HEAD_DIM), F32),
                        pltpu.VMEM((nbp, BLK), F32)] + _pipeline_slots(),
    )
    return pl.pallas_call(
        functools.partial(_moba_kernel, nb=nb, n_sel=n_sel),
        grid_spec=grid_spec,
        out_shape=jax.ShapeDtypeStruct((s, HALF), BF16),
        compiler_params=_cparams(2),
        name="moba_attention",
    )(slopes, z, z, v_t, e_k, e_q)


def _sortable(x):
    b = pltpu.bitcast(x, I32)
    return jnp.where(b < 0, b ^ 0x7FFFFFFF, b)


def _dsa_kernel(slopes_ref, qi_ref, wi_ref, ki_ref, q_ref, k_ref, vt_ref, ek_ref, eq_ref, o_ref,
                key_ref, half_ref, qm_ref, *slot_refs, n_sel):
    qt = pl.program_id(0)
    h = pl.program_id(1)
    jj = lax.broadcasted_iota(I32, (BLK, BLK), 0)
    ii = lax.broadcasted_iota(I32, (BLK, BLK), 1)
    causal = jj <= ii
    n_chunks = (qt + CHUNK) // CHUNK

    @pl.when(h == 0)
    def _():
        w_t = jnp.transpose(wi_ref[...].astype(F32))[0:IDX_HEADS, :]
        w_t = w_t * (IDX_HEADS ** -0.5 * IDX_DIM ** -0.5)
        lane = lax.broadcasted_iota(I32, (BLK, 128), 1)
        for hp in range(IDX_HEADS // 2):
            pair = qi_ref[:, hp * 128:(hp + 1) * 128]
            qm_ref[2 * hp] = jnp.where(lane < IDX_DIM, pair, jnp.zeros_like(pair))
            qm_ref[2 * hp + 1] = jnp.where(lane >= IDX_DIM, pair, jnp.zeros_like(pair))
        def score_body(kb, carry):
            ki_blk = ki_ref[pl.ds(pl.multiple_of(kb * BLK, BLK), BLK), :]
            sc = jnp.zeros((BLK, BLK), F32)
            for hh in range(IDX_HEADS):
                lg = _scores_t(ki_blk, qm_ref[hh])
                sc = sc + w_t[hh:hh + 1, :] * jnp.maximum(lg, 0.0)
            keys = jnp.where((kb < qt) | causal, _sortable(sc), INT_MIN)
            rows = pl.ds(pl.multiple_of(kb * BLK, BLK), BLK)
            key_ref[rows, :] = keys
            half_ref[rows, :] = lax.shift_right_arithmetic(keys, 16).astype(I16)
            return carry

        lax.fori_loop(0, qt + 1, score_body, 0)

        def fill_body(kb, carry):
            rows = pl.ds(pl.multiple_of(kb * BLK, BLK), BLK)
            key_ref[rows, :] = jnp.full((BLK, BLK), INT_MIN, I32)
            half_ref[rows, :] = jnp.full((BLK, BLK), I16_MIN, I16)
            return carry

        lax.fori_loop(qt + 1, n_chunks * CHUNK, fill_body, 0)

        def count_ge16(cand):
            def cbody(c, cnts):
                blk = half_ref[pl.ds(pl.multiple_of(c * (CHUNK * BLK), CHUNK * BLK), CHUNK * BLK), :]
                ind = jnp.where(blk >= cand, jnp.int16(1), jnp.int16(0))
                cnts = list(cnts)
                for r in range(CHUNK * BLK // 16):
                    cnts[r % len(cnts)] = cnts[r % len(cnts)] + ind[r * 16:(r + 1) * 16]
                return tuple(cnts)
            cnts = lax.fori_loop(0, n_chunks, cbody, tuple(jnp.zeros((16, BLK), I16) for _ in range(4)))
            tot = cnts[0].astype(I32) + cnts[1].astype(I32) + cnts[2].astype(I32) + cnts[3].astype(I32)
            return jnp.sum(tot, axis=0, keepdims=True)

        def bisect16(need):
            def bit_body(b, tb):
                cand_b = tb | lax.shift_left(jnp.int32(1), 15 - b)
                cnt = count_ge16((cand_b + I16_MIN).astype(I16))
                return jnp.where(cnt >= need, cand_b, tb)
            return lax.fori_loop(0, 16, bit_body, jnp.zeros((1, BLK), I32))

        t_hi = bisect16(n_sel) + I16_MIN
        above = count_ge16(jnp.minimum(t_hi + 1, -I16_MIN - 1).astype(I16))
        above = jnp.where(t_hi == -I16_MIN - 1, 0, above)

        def low_body(c, carry):
            rows = pl.ds(pl.multiple_of(c * (CHUNK * BLK), CHUNK * BLK), CHUNK * BLK)
            keys = key_ref[rows, :]
            low = (keys & 0xFFFF) + I16_MIN
            same = lax.shift_right_arithmetic(keys, 16) == t_hi
            half_ref[rows, :] = jnp.where(same, low, I16_MIN).astype(I16)
            return carry

        lax.fori_loop(0, n_chunks, low_body, 0)
        t_lo = bisect16(n_sel - above)
        thr = jnp.maximum(lax.shift_left(t_hi, 16) + t_lo, INT_MIN + 1)

        def mask_body(c, carry):
            rows = pl.ds(pl.multiple_of(c * (CHUNK * BLK), CHUNK * BLK), CHUNK * BLK)
            key_ref[rows, :] = jnp.where(key_ref[rows, :] >= thr, 0, NEG_BITS)
            return carry

        lax.fori_loop(0, n_chunks, mask_body, 0)

    slope2 = slopes_ref[h] * LOG2E
    q_ext = jnp.concatenate([q_ref[...], eq_ref[0]], axis=1)
    e_k = ek_ref[0]

    no_flags = [None] * CHUNK

    def chunk_meta(c):
        kbs = [jnp.minimum(c, n_chunks - 1) * CHUNK + b for b in range(CHUNK)]
        return kbs, [jnp.maximum(qt - kb, 0).astype(F32) * (-slope2 * BLK) for kb in kbs], no_flags

    def score_tile(kbs, b):
        rows = pl.ds(pl.multiple_of(kbs[b] * BLK, BLK), BLK)
        k_ext = jnp.concatenate([k_ref[rows, :], e_k], axis=1)
        return _scores_t(k_ext, q_ext) + pltpu.bitcast(key_ref[rows, :], F32)

    def value_tile(kbs, b):
        return vt_ref[0, kbs[b]]

    out_t = _attend(n_chunks, chunk_meta, score_tile, value_tile, (slot_refs,))
    o_ref[...] = jnp.transpose(out_t).astype(BF16)


def _dsa(z, v_t, slopes, e_k, e_q):
    s = z.shape[0]
    nb = s // BLK
    assert nb % CHUNK == 0
    n_sel = min(DSA_TOPK_MAX, s // 4)
    grid_spec = pltpu.PrefetchScalarGridSpec(
        num_scalar_prefetch=1,
        grid=(nb, N_HEADS),
        in_specs=[pl.BlockSpec((BLK, IDX_HEADS * IDX_DIM), lambda i, h, sl: (i, OD_QI // (IDX_HEADS * IDX_DIM))),
                  pl.BlockSpec((BLK, 128), lambda i, h, sl: (i, OD_WI // 128)),
                  pl.BlockSpec((s, 128), lambda i, h, sl: (0, OD_KI // 128)),
                  pl.BlockSpec((BLK, HEAD_DIM), lambda i, h, sl: (i, h)),
                  pl.BlockSpec((s, HEAD_DIM), lambda i, h, sl: (0, N_HEADS + h)),
                  pl.BlockSpec((1, nb, V_ROWS, BLK), lambda i, h, sl: (h, 0, 0, 0)),
                  pl.BlockSpec((1, BLK, HEAD_DIM), lambda i, h, sl: (h, 0, 0)),
                  pl.BlockSpec((1, BLK, HEAD_DIM), lambda i, h, sl: (h, 0, 0))],
        out_specs=pl.BlockSpec((BLK, HEAD_DIM), lambda i, h, sl: (i, h)),
        scratch_shapes=[pltpu.VMEM((s, BLK), I32),
                        pltpu.VMEM((s, BLK), I16),
                        pltpu.VMEM((IDX_HEADS, BLK, 128), BF16)] + _pipeline_slots(),
    )
    return pl.pallas_call(
        functools.partial(_dsa_kernel, n_sel=n_sel),
        grid_spec=grid_spec,
        out_shape=jax.ShapeDtypeStruct((s, HALF), BF16),
        compiler_params=_cparams(2),
        name="dsa_attention",
    )(slopes, z, z, z, z, z, v_t, e_k, e_q)


def _v_transposed(z, col0):
    s = z.shape[0]
    nb = s // BLK
    v = z[:, col0:col0 + HALF].reshape(nb, BLK, N_HEADS, HEAD_DIM)
    v_t = jnp.transpose(v, (2, 0, 3, 1))
    ones = jnp.ones((N_HEADS, nb, 8, BLK), v_t.dtype)
    return jnp.concatenate([v_t, ones, jnp.zeros((N_HEADS, nb, V_ROWS - HEAD_DIM - 8, BLK), v_t.dtype)], axis=2)


def _qk_gain(qn, kn, n):
    gq = jnp.tile(qn.astype(F32), N_HEADS) * (HEAD_DIM ** -0.5 * LOG2E)
    gk = jnp.tile(kn.astype(F32), N_HEADS)
    return jnp.concatenate([gq, gk, jnp.ones((n - 2 * HALF,), F32)])[None, :]


def kernel(x, c, norm_g, ada_w, ada_b, w_out, ev_w_in, pool_w, pool_scale, moba_qn, moba_kn,
           od_w_in, dsa_qn, dsa_kn, conv_w, conv_b, conv_ln_g, conv_ln_b):
    b, s, d = x.shape
    assert b == 1 and d == D_MODEL and s % BLK == 0
    depth = norm_g.shape[0]
    x2 = x[0].astype(F32)
    mods = _mods(c, ada_w, ada_b)[:, 0:1, :]
    slopes = jnp.exp2(-8.0 * jnp.arange(1, N_HEADS + 1, dtype=F32) / N_HEADS)
    e_k, e_q = _alibi_operands(slopes)

    for l in range(depth):
        shift, scl, gate_res = (mods[l, :, i * d:(i + 1) * d] for i in range(3))
        g = norm_g[l][None, :].astype(F32)
        if l % 2 == 0:
            e = l // 2
            w = ev_w_in[e]
            w = jnp.concatenate([w[:, HALF:4 * HALF], w[:, 0:HALF], w[:, 4 * HALF:]], axis=1).astype(BF16)
            z = _proj(x2, g, scl, shift, w, _qk_gain(moba_qn[e], moba_kn[e], EV_N), n_norm_cols=2 * HALF)
            ya = _pool(z, 3, pool_w[e].astype(BF16), pool_scale[e][None, :].astype(F32))
            yb = _moba(z, N_HEADS, _v_transposed(z, 2 * HALF), slopes, e_k, e_q)
            g_blk = 4
        else:
            o = l // 2
            w = od_w_in[o]
            c_qi = 3 * HALF
            c_wi = c_qi + IDX_HEADS * IDX_DIM
            c_ki = c_wi + IDX_HEADS
            c_ga = c_ki + IDX_DIM
            ki_w = w[:, c_ki:c_ga]
            w = jnp.concatenate([
                w[:, 0:3 * HALF], w[:, c_ga:],
                w[:, c_qi:c_wi], ki_w, ki_w,
                w[:, c_wi:c_ki], jnp.zeros((d, OD_N - OD_WI - IDX_HEADS), w.dtype)], axis=1).astype(BF16)
            z = _proj(x2, g, scl, shift, w, _qk_gain(dsa_qn[o], dsa_kn[o], OD_N), n_norm_cols=2 * HALF)
            ya = _dsa(z, _v_transposed(z, 2 * HALF), slopes, e_k, e_q)
            yb = _conv(z, 3, 4, conv_w[o].astype(F32), conv_b[o][None, :].astype(F32),
                       conv_ln_g[o][None, :].astype(F32), conv_ln_b[o][None, :].astype(F32))
            g_blk = 5
        x2 = _outproj(ya, yb, z, g_blk, x2, gate_res, w_out[l].astype(BF16))
    return x2[None].astype(x.dtype)
```

```python
import functools

import jax
import numpy as np
import jax.numpy as jnp
from jax import lax
from jax.experimental import pallas as pl
from jax.experimental.pallas import tpu as pltpu

F32 = jnp.float32
BF16 = jnp.bfloat16
I32 = jnp.int32
I16 = jnp.int16

D_MODEL = 2048
HALF = D_MODEL // 2
HEAD_DIM = 128
N_HEADS = HALF // HEAD_DIM
POOL_WINDOWS = (2, 4, 8, 16)
POOL_GW = HALF // len(POOL_WINDOWS)
POOL_HALO = 16
BLK = 256
CHUNK = 4
MOBA_TOPK = 3
DSA_TOPK_MAX = 256
IDX_HEADS = 8
IDX_DIM = 64
CONV_WIDTH = 31
CONV_HALO = 32
SUBLANES = 8
EPS = 1e-6
NEG = -1e30
LOG2E = 1.4426950408889634
INT_MIN = -(2 ** 31)
I16_MIN = -(2 ** 15)
NEG_BITS = int(np.array(NEG, np.float32).view(np.int32))
V_ROWS = HEAD_DIM + 16

EV_N = 4 * HALF + D_MODEL
OD_N = 8192
OD_QI = 7 * HALF
OD_KI = OD_QI + IDX_HEADS * IDX_DIM
OD_WI = OD_KI + 128

VMEM_LIMIT = 56 * 1024 * 1024


def _cparams(n_axes, vmem=VMEM_LIMIT):
    return pltpu.CompilerParams(dimension_semantics=("arbitrary",) * n_axes, vmem_limit_bytes=vmem)


def _sigmoid(x):
    return 1.0 / (1.0 + jnp.exp(-x))


def _mods_kernel(c_ref, w_ref, b_ref, o_ref):
    c = c_ref[...]
    ca = c * _sigmoid(c)
    o_ref[0] = jnp.dot(ca, w_ref[0], precision=lax.Precision.HIGHEST,
                       preferred_element_type=F32) + b_ref[0]


def _mods(c, ada_w, ada_b):
    depth, d, n = ada_w.shape
    tn = 768
    c8 = jnp.broadcast_to(c.astype(F32), (8, d))
    return pl.pallas_call(
        _mods_kernel,
        grid=(depth, n // tn),
        in_specs=[pl.BlockSpec((8, d), lambda l, j: (0, 0)),
                  pl.BlockSpec((1, d, tn), lambda l, j: (l, 0, j)),
                  pl.BlockSpec((1, 1, tn), lambda l, j: (l, 0, j))],
        out_specs=pl.BlockSpec((1, 8, tn), lambda l, j: (l, 0, j)),
        out_shape=jax.ShapeDtypeStruct((depth, 8, n), F32),
        compiler_params=_cparams(2),
        name="adaln_mods",
    )(c8, ada_w, ada_b.reshape(depth, 1, n))


def _proj_kernel(x_ref, g_ref, scl_ref, sh_ref, w_ref, gain_ref, o_ref, h_ref, *, norm_tiles, tn):
    j = pl.program_id(1)

    @pl.when(j == 0)
    def _():
        x = x_ref[...]
        y = x * lax.rsqrt(jnp.mean(x * x, axis=-1, keepdims=True) + EPS)
        h = (y * g_ref[...]) * (1.0 + scl_ref[...]) + sh_ref[...]
        h_ref[...] = h.astype(BF16)

    z = jnp.dot(h_ref[...], w_ref[...], preferred_element_type=F32)

    is_qk = jnp.logical_and(j >= norm_tiles[0], j < norm_tiles[1])

    @pl.when(is_qk)
    def _():
        for c in range(tn // HEAD_DIM):
            cs = slice(c * HEAD_DIM, (c + 1) * HEAD_DIM)
            zc = z[:, cs]
            r = lax.rsqrt(jnp.mean(zc * zc, axis=-1, keepdims=True) + EPS)
            o_ref[:, cs] = (zc * r * gain_ref[:, cs]).astype(BF16)

    @pl.when(jnp.logical_not(is_qk))
    def _():
        o_ref[...] = z.astype(BF16)


def _proj(x2, g, scl, sh, w, gain, *, qk_col0):
    s, d = x2.shape
    n = w.shape[1]
    tm = min(1024, s)
    tn = 512
    return pl.pallas_call(
        functools.partial(_proj_kernel, norm_tiles=(qk_col0 // tn, (qk_col0 + 2 * HALF) // tn), tn=tn),
        grid=(s // tm, n // tn),
        in_specs=[pl.BlockSpec((tm, d), lambda i, j: (i, 0)),
                  pl.BlockSpec((1, d), lambda i, j: (0, 0)),
                  pl.BlockSpec((1, d), lambda i, j: (0, 0)),
                  pl.BlockSpec((1, d), lambda i, j: (0, 0)),
                  pl.BlockSpec((d, tn), lambda i, j: (0, j)),
                  pl.BlockSpec((1, tn), lambda i, j: (0, j))],
        out_specs=pl.BlockSpec((tm, tn), lambda i, j: (i, j)),
        out_shape=jax.ShapeDtypeStruct((s, n), BF16),
        scratch_shapes=[pltpu.VMEM((tm, d), BF16)],
        compiler_params=_cparams(2),
        name="norm_proj",
    )(x2, g, scl, sh, w, gain)


def _out_kernel(ya_ref, yb_ref, ga_ref, gb_ref, x_ref, gr_ref, w_ref, o_ref, a_ref):
    j = pl.program_id(1)

    @pl.when(j == 0)
    def _():
        for y_ref, g_ref, cs in ((ya_ref, ga_ref, slice(0, HALF)), (yb_ref, gb_ref, slice(HALF, 2 * HALF))):
            g = g_ref[...].astype(F32)
            a_ref[:, cs] = (y_ref[...].astype(F32) * (g * _sigmoid(g))).astype(BF16)

    acc = jnp.dot(a_ref[...], w_ref[...], preferred_element_type=F32)
    o_ref[...] = x_ref[...] + gr_ref[...] * acc


def _outproj(ya, yb, z, g_col_blk, x2, gate_res, w):
    s, d = x2.shape
    tm = min(1024, s)
    tn = 512
    return pl.pallas_call(
        _out_kernel,
        grid=(s // tm, d // tn),
        in_specs=[pl.BlockSpec((tm, HALF), lambda i, j: (i, 0)),
                  pl.BlockSpec((tm, HALF), lambda i, j: (i, 0)),
                  pl.BlockSpec((tm, HALF), lambda i, j: (i, g_col_blk)),
                  pl.BlockSpec((tm, HALF), lambda i, j: (i, g_col_blk + 1)),
                  pl.BlockSpec((tm, tn), lambda i, j: (i, j)),
                  pl.BlockSpec((1, tn), lambda i, j: (0, j)),
                  pl.BlockSpec((2 * HALF, tn), lambda i, j: (0, j))],
        out_specs=pl.BlockSpec((tm, tn), lambda i, j: (i, j)),
        out_shape=jax.ShapeDtypeStruct((s, d), F32),
        scratch_shapes=[pltpu.VMEM((tm, 2 * HALF), BF16)],
        compiler_params=_cparams(2),
        name="gated_outproj",
    )(ya, yb, z, z, x2, gate_res, w)


def _pool_kernel(u_ref, halo_ref, w_ref, ps_ref, o_ref, ext_ref, *, tm):
    i = pl.program_id(0)
    halo = halo_ref[...].astype(F32)
    ext_ref[0:POOL_HALO, :] = jnp.where(i > 0, halo, 0.0)
    ext_ref[POOL_HALO:POOL_HALO + tm, :] = u_ref[...].astype(F32)
    t1 = (i * tm + 1 + lax.broadcasted_iota(I32, (tm, 1), 0)).astype(F32)
    for g, win in enumerate(POOL_WINDOWS):
        cs = slice(g * POOL_GW, (g + 1) * POOL_GW)
        u = ext_ref[POOL_HALO:POOL_HALO + tm, cs]
        acc = u
        for j in range(1, win):
            acc = acc + ext_ref[POOL_HALO - j:POOL_HALO - j + tm, cs]
        dlt = acc / jnp.minimum(t1, float(win)) - u
        y = jnp.dot(dlt.astype(BF16), w_ref[g], preferred_element_type=F32)
        o_ref[:, cs] = (y * ps_ref[:, cs]).astype(BF16)


def _pool(z, u_col_blk, pool_w, pool_scale):
    s = z.shape[0]
    tm = min(512, s)
    hb = tm // POOL_HALO
    return pl.pallas_call(
        functools.partial(_pool_kernel, tm=tm),
        grid=(s // tm,),
        in_specs=[pl.BlockSpec((tm, HALF), lambda i: (i, u_col_blk)),
                  pl.BlockSpec((POOL_HALO, HALF),
                               lambda i: (jnp.maximum(i * hb - 1, 0), u_col_blk)),
                  pl.BlockSpec((len(POOL_WINDOWS), POOL_GW, POOL_GW), lambda i: (0, 0, 0)),
                  pl.BlockSpec((1, HALF), lambda i: (0, 0))],
        out_specs=pl.BlockSpec((tm, HALF), lambda i: (i, 0)),
        out_shape=jax.ShapeDtypeStruct((s, HALF), BF16),
        scratch_shapes=[pltpu.VMEM((POOL_HALO + tm, HALF), F32)],
        compiler_params=_cparams(1),
        name="pool_mixer",
    )(z, z, pool_w, pool_scale)


def _conv_kernel(a_ref, b_ref, ah_ref, bh_ref, cw_ref, cb_ref, lg_ref, lb_ref, o_ref,
                 ext_ref, sh_ref, y_ref, *, tm):
    i = pl.program_id(0)

    def glu(a, b):
        return a.astype(F32) * _sigmoid(b.astype(F32))

    ext_ref[0:CONV_HALO, :] = jnp.where(i > 0, glu(ah_ref[...], bh_ref[...]), 0.0)
    ext_ref[CONV_HALO:CONV_HALO + tm, :] = glu(a_ref[...], b_ref[...])
    span = CONV_HALO + tm - SUBLANES
    for r in range(1, SUBLANES):
        sh_ref[r - 1, 0:span, :] = ext_ref[r:r + span, :]
    off = CONV_HALO - (CONV_WIDTH - 1)
    for ct in range(HALF // 128):
        cs = slice(ct * 128, (ct + 1) * 128)
        acc = None
        for j in range(CONV_WIDTH):
            r, base = (off + j) % SUBLANES, (off + j) // SUBLANES * SUBLANES
            src = ext_ref[base:base + tm, cs] if r == 0 else sh_ref[r - 1, base:base + tm, cs]
            term = cw_ref[j:j + 1, cs] * src
            acc = term if acc is None else acc + term
        y_ref[:, cs] = acc + cb_ref[:, cs]
    y = y_ref[...]
    mu = jnp.mean(y, axis=-1, keepdims=True)
    yc = y - mu
    var = jnp.mean(yc * yc, axis=-1, keepdims=True)
    yn = yc * lax.rsqrt(var + EPS) * lg_ref[...] + lb_ref[...]
    o_ref[...] = (yn * _sigmoid(yn)).astype(BF16)


def _conv(z, a_col_blk, b_col_blk, conv_w, conv_b, ln_g, ln_b):
    s = z.shape[0]
    tm = min(256, s)
    hb = tm // CONV_HALO
    cwp = jnp.zeros((32, HALF), F32).at[:CONV_WIDTH].set(conv_w)
    return pl.pallas_call(
        functools.partial(_conv_kernel, tm=tm),
        grid=(s // tm,),
        in_specs=[pl.BlockSpec((tm, HALF), lambda i: (i, a_col_blk)),
                  pl.BlockSpec((tm, HALF), lambda i: (i, b_col_blk)),
                  pl.BlockSpec((CONV_HALO, HALF), lambda i: (jnp.maximum(i * hb - 1, 0), a_col_blk)),
                  pl.BlockSpec((CONV_HALO, HALF), lambda i: (jnp.maximum(i * hb - 1, 0), b_col_blk)),
                  pl.BlockSpec((32, HALF), lambda i: (0, 0)),
                  pl.BlockSpec((1, HALF), lambda i: (0, 0)),
                  pl.BlockSpec((1, HALF), lambda i: (0, 0)),
                  pl.BlockSpec((1, HALF), lambda i: (0, 0))],
        out_specs=pl.BlockSpec((tm, HALF), lambda i: (i, 0)),
        out_shape=jax.ShapeDtypeStruct((s, HALF), BF16),
        scratch_shapes=[pltpu.VMEM((CONV_HALO + tm, HALF), F32),
                        pltpu.VMEM((SUBLANES - 1, CONV_HALO + tm, HALF), F32),
                        pltpu.VMEM((tm, HALF), F32)],
        compiler_params=_cparams(1),
        name="conv_module",
    )(z, z, z, z, cwp, conv_b, ln_g, ln_b)


def _alibi_operands(slopes):
    s2 = slopes.astype(F32) * LOG2E
    pieces = []
    rest = s2
    for _ in range(3):
        part = rest.astype(BF16).astype(F32)
        pieces.append(part)
        rest = rest - part
    pieces = jnp.broadcast_to(jnp.stack(pieces, axis=-1)[:, None, :], (N_HEADS, BLK, 3))
    pos = jnp.broadcast_to(jnp.arange(BLK, dtype=F32)[None, :, None], (N_HEADS, BLK, 3))
    pad = jnp.zeros((N_HEADS, BLK, HEAD_DIM - 6), F32)
    e_k = jnp.concatenate([pos, pieces, pad], axis=-1).astype(BF16)
    e_q = jnp.concatenate([pieces, -pos, pad], axis=-1).astype(BF16)
    return e_k, e_q


def _causal_tile():
    jj = lax.broadcasted_iota(I32, (BLK, BLK), 0)
    ii = lax.broadcasted_iota(I32, (BLK, BLK), 1)
    return jj <= ii


def _pipe_step(ops, m, acc, x=None, y=None, z=None):
    meta, score_tile, value_tile, valid = ops
    if x is not None:
        ix, s_x = x
        state_x, c2_x, flag_x, _ = meta(ix)
    if y is not None:
        iy, s_y, p_y, bm_y = y
        _, c2_y, flag_y, reset = meta(iy)
        m_old = m if reset is None else jnp.where(reset, NEG, m)
        m = jnp.maximum(m_old, bm_y)
        alpha_y = jnp.exp2(m_old - m)
    if z is not None:
        iz, p_z, alpha_z = z
        state_z, _, _, _ = meta(iz)
    bm_x = None
    pv = None
    for b in range(CHUNK):
        rows = slice(b * BLK, (b + 1) * BLK)
        if x is not None:
            t = score_tile(state_x, b)
            s_x[rows, :] = t
            bm = jnp.max(t, axis=0, keepdims=True) + c2_x[b]
            if flag_x[b] is not None:
                bm = jnp.where(flag_x[b], bm, NEG)
            bm_x = bm if bm_x is None else jnp.maximum(bm_x, bm)
        if y is not None:
            sub = m - c2_y[b]
            if flag_y[b] is not None:
                sub = jnp.where(flag_y[b], sub, -NEG)
            p_y[rows, :] = jnp.exp2(s_y[rows, :] - sub).astype(BF16)
        if z is not None:
            d = jnp.dot(value_tile(state_z, b), p_z[rows, :], preferred_element_type=F32)
            pv = d if pv is None else pv + d
    out = {}
    if x is not None:
        out["bm"] = jnp.where(valid(ix), bm_x, NEG)
    if y is not None:
        out["alpha"] = alpha_y
    if z is not None:
        acc = alpha_z * acc + pv
    return m, acc, out


def _load_state(state_ref):
    return tuple(state_ref[k, 0:1, :] for k in range(4))


def _store_state(state_ref, m, bm1, bm2, alpha):
    for k, v in enumerate((m, bm1, bm2, alpha)):
        state_ref[k] = jnp.broadcast_to(v, (SUBLANES, BLK))


def _pipe_trip(ops, slots, first_item, acc, emit=None):
    s_slots, p_slots, state_ref = slots[:4], slots[4:6], slots[6]
    m, bm1, bm2, alpha = _load_state(state_ref)
    for u in range(4):
        i = first_item + u
        m, acc, o = _pipe_step(ops, m, acc,
                               x=(i + 3, s_slots[(u + 3) % 4]),
                               y=(i + 1, s_slots[(u + 1) % 4], p_slots[(u + 1) % 2], bm1),
                               z=(i, p_slots[u % 2], alpha))
        if emit is not None:
            emit(i, acc)
        bm1, bm2, alpha = bm2, o["bm"], o["alpha"]
    _store_state(state_ref, m, bm1, bm2, alpha)
    return acc


def _pipe_fill(ops, slots):
    s_slots, p_slots, state_ref = slots[:4], slots[4:6], slots[6]
    m = jnp.full((1, BLK), NEG, F32)
    acc = jnp.zeros((V_ROWS, BLK), F32)
    m, acc, o0 = _pipe_step(ops, m, acc, x=(0, s_slots[0]))
    m, acc, o1 = _pipe_step(ops, m, acc, x=(1, s_slots[1]))
    m, acc, o2 = _pipe_step(ops, m, acc, x=(2, s_slots[2]), y=(0, s_slots[0], p_slots[0], o0["bm"]))
    _store_state(state_ref, m, o1["bm"], o2["bm"], o2["alpha"])
    return acc


def _attend(n, ops, slots):
    s_slots, p_slots, state_ref = slots[:4], slots[4:6], slots[6]
    acc = lax.fori_loop(0, n // 4, lambda j, a: _pipe_trip(ops, slots, 4 * j, a), _pipe_fill(ops, slots))
    base = (n // 4) * 4
    for u in range(3):
        def tail(acc=acc, u=u):
            m, bm1, bm2, alpha = _load_state(state_ref)
            m, acc_new, o = _pipe_step(ops, m, acc,
                                       y=(base + u + 1, s_slots[(u + 1) % 4], p_slots[(u + 1) % 2], bm1),
                                       z=(base + u, p_slots[u % 2], alpha))
            _store_state(state_ref, m, bm2, jnp.full((1, BLK), NEG, F32), o["alpha"])
            return acc_new
        acc = lax.cond(n % 4 > u, tail, lambda acc=acc: acc)
    return acc[0:HEAD_DIM] / acc[HEAD_DIM:HEAD_DIM + 1]


def _pipeline_slots():
    return ([pltpu.VMEM((CHUNK * BLK, BLK), F32)] * 4 + [pltpu.VMEM((CHUNK * BLK, BLK), BF16)] * 2
            + [pltpu.VMEM((4, SUBLANES, BLK), F32)])


def _scores_t(k_blk, q):
    return lax.dot_general(k_blk, q, (((1,), (1,)), ((), ())), preferred_element_type=F32)


def _moba_kernel(slopes_ref, q_ref, k_ref, vt_ref, ek_ref, eq_ref, o_ref, km_ref, sel_ref,
                 *slot_refs, nb, n_sel):
    h = pl.program_id(0)
    qi = pl.program_id(1)
    nbp = sel_ref.shape[0]

    @pl.when(qi == 0)
    def _():
        km_ref[...] = jnp.zeros_like(km_ref)

        def mean_body(b, carry):
            blk = k_ref[pl.ds(pl.multiple_of(b * BLK, BLK), BLK), :].astype(F32)
            km_ref[pl.ds(b, 1), :] = jnp.sum(blk, axis=0, keepdims=True) * (1.0 / BLK)
            return carry

        lax.fori_loop(0, nb, mean_body, 0)

    slope2 = slopes_ref[h] * LOG2E
    causal = _causal_tile()
    q = q_ref[...]
    q_ext = jnp.concatenate([q, eq_ref[0]], axis=1)
    e_k = ek_ref[0]

    g_t = lax.dot_general(km_ref[...], q.astype(F32), (((1,), (1,)), ((), ())),
                          precision=lax.Precision.HIGHEST, preferred_element_type=F32)
    row = lax.broadcasted_iota(I32, (nbp, BLK), 0)
    past = row < qi
    cur = jnp.where(past, g_t, NEG)
    picked = jnp.zeros((nbp, BLK), jnp.bool_)
    for _ in range(n_sel):
        mx = jnp.max(cur, axis=0, keepdims=True)
        first = jnp.min(jnp.where(cur == mx, row, nbp), axis=0, keepdims=True)
        hit = row == first
        picked = picked | hit
        cur = jnp.where(hit, -3e38, cur)
    sel_ref[...] = jnp.where(picked & past, 1.0, 0.0)

    n_main = qi // CHUNK

    def meta(c):
        c = jnp.minimum(c, n_main)
        tail = c == n_main
        kbs, c2s, flags = [], [], []
        for b in range(CHUNK):
            raw = jnp.where(tail, qi - (CHUNK - 1) + b, c * CHUNK + b)
            kb = jnp.clip(raw, 0, nb - 1)
            enabled = jnp.logical_or(jnp.logical_not(tail), raw >= n_main * CHUNK)
            sel_row = jnp.where(enabled, sel_ref[pl.ds(kb, 1), :], 0.0)
            if b == CHUNK - 1:
                sel_row = jnp.where(tail, 1.0, sel_row)
            kbs.append(kb)
            c2s.append((qi - kb).astype(F32) * (-slope2 * BLK))
            flags.append(sel_row > 0.5)
        return (tail, kbs), c2s, flags, None

    def score_tile(state, b):
        tail, kbs = state
        k_blk = k_ref[pl.ds(pl.multiple_of(kbs[b] * BLK, BLK), BLK), :]
        t = _scores_t(jnp.concatenate([k_blk, e_k], axis=1), q_ext)
        if b == CHUNK - 1:
            t = jnp.where(tail, jnp.where(causal, t, NEG), t)
        return t

    def value_tile(state, b):
        return vt_ref[0, state[1][b]]

    out_t = _attend(n_main + 1, (meta, score_tile, value_tile, lambda c: c <= n_main), slot_refs)
    o_ref[...] = jnp.transpose(out_t).astype(BF16)


def _moba(z, q_col_blk0, k_col_blk0, v_t, slopes, e_k, e_q):
    s = z.shape[0]
    nb = s // BLK
    assert nb % CHUNK == 0
    nbp = -(-nb // 8) * 8
    n_sel = max(1, min(MOBA_TOPK, nb - 1))
    grid_spec = pltpu.PrefetchScalarGridSpec(
        num_scalar_prefetch=1,
        grid=(N_HEADS, nb),
        in_specs=[pl.BlockSpec((BLK, HEAD_DIM), lambda h, i, sl: (i, q_col_blk0 + h)),
                  pl.BlockSpec((s, HEAD_DIM), lambda h, i, sl: (0, k_col_blk0 + h)),
                  pl.BlockSpec((1, nb, V_ROWS, BLK), lambda h, i, sl: (h, 0, 0, 0)),
                  pl.BlockSpec((1, BLK, HEAD_DIM), lambda h, i, sl: (h, 0, 0)),
                  pl.BlockSpec((1, BLK, HEAD_DIM), lambda h, i, sl: (h, 0, 0))],
        out_specs=pl.BlockSpec((BLK, HEAD_DIM), lambda h, i, sl: (i, h)),
        scratch_shapes=[pltpu.VMEM((nbp, HEAD_DIM), F32),
                        pltpu.VMEM((nbp, BLK), F32)] + _pipeline_slots(),
    )
    return pl.pallas_call(
        functools.partial(_moba_kernel, nb=nb, n_sel=n_sel),
        grid_spec=grid_spec,
        out_shape=jax.ShapeDtypeStruct((s, HALF), BF16),
        compiler_params=_cparams(2),
        name="moba_attention",
    )(slopes, z, z, v_t, e_k, e_q)


def _sortable(x):
    b = pltpu.bitcast(x, I32)
    return jnp.where(b < 0, b ^ 0x7FFFFFFF, b)


def _dsa_kernel(slopes_ref, qi_ref, wi_ref, ki_ref, q_ref, k_ref, vt_ref, ek_ref, eq_ref, o_ref,
                key_ref, half_ref, qm_ref, *slot_refs, n_sel):
    qt = pl.program_id(0)
    h = pl.program_id(1)
    causal = _causal_tile()
    n_chunks = (qt + CHUNK) // CHUNK

    @pl.when(h == 0)
    def _():
        w_t = jnp.transpose(wi_ref[...].astype(F32))[0:IDX_HEADS, :]
        w_t = w_t * (IDX_HEADS ** -0.5 * IDX_DIM ** -0.5)
        lane = lax.broadcasted_iota(I32, (BLK, 128), 1)
        for hp in range(IDX_HEADS // 2):
            pair = qi_ref[:, hp * 128:(hp + 1) * 128]
            qm_ref[2 * hp] = jnp.where(lane < IDX_DIM, pair, jnp.zeros_like(pair))
            qm_ref[2 * hp + 1] = jnp.where(lane >= IDX_DIM, pair, jnp.zeros_like(pair))

        def score_body(c, carry):
            rows_c = pl.ds(pl.multiple_of(c * (CHUNK * BLK), CHUNK * BLK), CHUNK * BLK)
            ki_blk = ki_ref[rows_c, :]
            sc = jnp.zeros((CHUNK * BLK, BLK), F32)
            for hh in range(IDX_HEADS):
                lg = _scores_t(ki_blk, qm_ref[hh])
                sc = sc + w_t[hh:hh + 1, :] * jnp.maximum(lg, 0.0)
            sk = _sortable(sc)
            for b in range(CHUNK):
                kb = c * CHUNK + b
                blk = sk[b * BLK:(b + 1) * BLK]
                on_diag = jnp.where(causal, blk, INT_MIN)
                keys = jnp.where(kb < qt, blk, jnp.where(kb == qt, on_diag, INT_MIN))
                rows = pl.ds(pl.multiple_of(kb * BLK, BLK), BLK)
                key_ref[rows, :] = keys
                half_ref[rows, :] = lax.shift_right_arithmetic(keys, 16).astype(I16)
            return carry

        lax.fori_loop(0, n_chunks, score_body, 0)

        def count_ge16(cand):
            def cbody(c, cnts):
                chunk = half_ref.at[pl.ds(pl.multiple_of(c * (CHUNK * BLK), CHUNK * BLK), CHUNK * BLK), :]
                cnts = list(cnts)
                for r in range(CHUNK * BLK // 16):
                    hit = jnp.where(chunk[r * 16:(r + 1) * 16, :] >= cand, jnp.int16(1), jnp.int16(0))
                    cnts[r % len(cnts)] = cnts[r % len(cnts)] + hit
                return tuple(cnts)
            cnts = lax.fori_loop(0, n_chunks, cbody, tuple(jnp.zeros((16, BLK), I16) for _ in range(4)))
            tot = cnts[0].astype(I32) + cnts[1].astype(I32) + cnts[2].astype(I32) + cnts[3].astype(I32)
            return jnp.sum(tot, axis=0, keepdims=True)

        def bisect16(need):
            def bit_body(b, tb):
                cand_b = tb | lax.shift_left(jnp.int32(1), 15 - b)
                cnt = count_ge16((cand_b + I16_MIN).astype(I16))
                return jnp.where(cnt >= need, cand_b, tb)
            return lax.fori_loop(0, 16, bit_body, jnp.zeros((1, BLK), I32))

        t_hi = bisect16(n_sel) + I16_MIN
        above = count_ge16(jnp.minimum(t_hi + 1, -I16_MIN - 1).astype(I16))
        above = jnp.where(t_hi == -I16_MIN - 1, 0, above)

        def low_body(c, carry):
            rows = pl.ds(pl.multiple_of(c * (CHUNK * BLK), CHUNK * BLK), CHUNK * BLK)
            keys = key_ref[rows, :]
            low = (keys & 0xFFFF) + I16_MIN
            same = lax.shift_right_arithmetic(keys, 16) == t_hi
            half_ref[rows, :] = jnp.where(same, low, I16_MIN).astype(I16)
            return carry

        lax.fori_loop(0, n_chunks, low_body, 0)
        t_lo = bisect16(n_sel - above)
        thr = jnp.maximum(lax.shift_left(t_hi, 16) + t_lo, INT_MIN + 1)

        def mask_body(c, carry):
            rows = pl.ds(pl.multiple_of(c * (CHUNK * BLK), CHUNK * BLK), CHUNK * BLK)
            key_ref[rows, :] = jnp.where(key_ref[rows, :] >= thr, 0, NEG_BITS)
            return carry

        lax.fori_loop(0, n_chunks, mask_body, 0)

    slope2 = slopes_ref[h] * LOG2E
    q_ext = jnp.concatenate([q_ref[...], eq_ref[0]], axis=1)
    e_k = ek_ref[0]

    no_flags = [None] * CHUNK

    def chunk_meta(c):
        kbs = [jnp.minimum(c, n_chunks - 1) * CHUNK + b for b in range(CHUNK)]
        return kbs, [jnp.maximum(qt - kb, 0).astype(F32) * (-slope2 * BLK) for kb in kbs], no_flags, None

    def score_tile(kbs, b):
        rows = pl.ds(pl.multiple_of(kbs[b] * BLK, BLK), BLK)
        k_ext = jnp.concatenate([k_ref[rows, :], e_k], axis=1)
        return _scores_t(k_ext, q_ext) + pltpu.bitcast(key_ref[rows, :], F32)

    def value_tile(kbs, b):
        return vt_ref[0, kbs[b]]

    out_t = _attend(n_chunks, (chunk_meta, score_tile, value_tile, lambda c: c < n_chunks), slot_refs)
    o_ref[...] = jnp.transpose(out_t).astype(BF16)


def _dsa(z, v_t, slopes, e_k, e_q):
    s = z.shape[0]
    nb = s // BLK
    assert nb % CHUNK == 0
    n_sel = min(DSA_TOPK_MAX, s // 4)
    grid_spec = pltpu.PrefetchScalarGridSpec(
        num_scalar_prefetch=1,
        grid=(nb, N_HEADS),
        in_specs=[pl.BlockSpec((BLK, IDX_HEADS * IDX_DIM), lambda i, h, sl: (i, OD_QI // (IDX_HEADS * IDX_DIM))),
                  pl.BlockSpec((BLK, 128), lambda i, h, sl: (i, OD_WI // 128)),
                  pl.BlockSpec((s, 128), lambda i, h, sl: (0, OD_KI // 128)),
                  pl.BlockSpec((BLK, HEAD_DIM), lambda i, h, sl: (i, h)),
                  pl.BlockSpec((s, HEAD_DIM), lambda i, h, sl: (0, N_HEADS + h)),
                  pl.BlockSpec((1, nb, V_ROWS, BLK), lambda i, h, sl: (h, 0, 0, 0)),
                  pl.BlockSpec((1, BLK, HEAD_DIM), lambda i, h, sl: (h, 0, 0)),
                  pl.BlockSpec((1, BLK, HEAD_DIM), lambda i, h, sl: (h, 0, 0))],
        out_specs=pl.BlockSpec((BLK, HEAD_DIM), lambda i, h, sl: (i, h)),
        scratch_shapes=[pltpu.VMEM((s, BLK), I32),
                        pltpu.VMEM((s, BLK), I16),
                        pltpu.VMEM((IDX_HEADS, BLK, 128), BF16)] + _pipeline_slots(),
    )
    return pl.pallas_call(
        functools.partial(_dsa_kernel, n_sel=n_sel),
        grid_spec=grid_spec,
        out_shape=jax.ShapeDtypeStruct((s, HALF), BF16),
        compiler_params=_cparams(2),
        name="dsa_attention",
    )(slopes, z, z, z, z, z, v_t, e_k, e_q)


def _v_transposed(z, col0):
    s = z.shape[0]
    nb = s // BLK
    v = z[:, col0:col0 + HALF].reshape(nb, BLK, N_HEADS, HEAD_DIM)
    v_t = jnp.transpose(v, (2, 0, 3, 1))
    ones = jnp.ones((N_HEADS, nb, 8, BLK), v_t.dtype)
    return jnp.concatenate([v_t, ones, jnp.zeros((N_HEADS, nb, V_ROWS - HEAD_DIM - 8, BLK), v_t.dtype)], axis=2)


def _qk_gain(qn, kn, col0, n):
    gq = jnp.tile(qn.astype(F32), N_HEADS) * (HEAD_DIM ** -0.5 * LOG2E)
    gk = jnp.tile(kn.astype(F32), N_HEADS)
    return jnp.concatenate([jnp.ones((col0,), F32), gq, gk, jnp.ones((n - col0 - 2 * HALF,), F32)])[None, :]


def kernel(x, c, norm_g, ada_w, ada_b, w_out, ev_w_in, pool_w, pool_scale, moba_qn, moba_kn,
           od_w_in, dsa_qn, dsa_kn, conv_w, conv_b, conv_ln_g, conv_ln_b):
    b, s, d = x.shape
    assert b == 1 and d == D_MODEL and s % BLK == 0
    depth = norm_g.shape[0]
    x2 = x[0].astype(F32)
    mods = _mods(c, ada_w, ada_b)[:, 0:1, :]
    slopes = jnp.exp2(-8.0 * jnp.arange(1, N_HEADS + 1, dtype=F32) / N_HEADS)
    e_k, e_q = _alibi_operands(slopes)

    for l in range(depth):
        shift, scl, gate_res = (mods[l, :, i * d:(i + 1) * d] for i in range(3))
        g = norm_g[l][None, :].astype(F32)
        if l % 2 == 0:
            e = l // 2
            w = ev_w_in[e].astype(BF16)
            z = _proj(x2, g, scl, shift, w, _qk_gain(moba_qn[e], moba_kn[e], HALF, EV_N), qk_col0=HALF)
            ya = _pool(z, 0, pool_w[e].astype(BF16), pool_scale[e][None, :].astype(F32))
            yb = _moba(z, N_HEADS, 2 * N_HEADS, _v_transposed(z, 3 * HALF), slopes, e_k, e_q)
            g_blk = 4
        else:
            o = l // 2
            w = od_w_in[o]
            c_qi = 3 * HALF
            c_wi = c_qi + IDX_HEADS * IDX_DIM
            c_ki = c_wi + IDX_HEADS
            c_ga = c_ki + IDX_DIM
            ki_w = w[:, c_ki:c_ga]
            w = jnp.concatenate([
                w[:, 0:3 * HALF], w[:, c_ga:],
                w[:, c_qi:c_wi], ki_w, ki_w,
                w[:, c_wi:c_ki], jnp.zeros((d, OD_N - OD_WI - IDX_HEADS), w.dtype)], axis=1).astype(BF16)
            z = _proj(x2, g, scl, shift, w, _qk_gain(dsa_qn[o], dsa_kn[o], 0, OD_N), qk_col0=0)
            ya = _dsa(z, _v_transposed(z, 2 * HALF), slopes, e_k, e_q)
            yb = _conv(z, 3, 4, conv_w[o].astype(F32), conv_b[o][None, :].astype(F32),
                       conv_ln_g[o][None, :].astype(F32), conv_ln_b[o][None, :].astype(F32))
            g_blk = 5
        x2 = _outproj(ya, yb, z, g_blk, x2, gate_res, w_out[l].astype(BF16))
    return x2[None].astype(x.dtype)
```

```python
import functools

import jax
import numpy as np
import jax.numpy as jnp
from jax import lax
from jax.experimental import pallas as pl
from jax.experimental.pallas import tpu as pltpu

F32 = jnp.float32
BF16 = jnp.bfloat16
I32 = jnp.int32
I16 = jnp.int16

D_MODEL = 2048
HALF = D_MODEL // 2
HEAD_DIM = 128
N_HEADS = HALF // HEAD_DIM
POOL_WINDOWS = (2, 4, 8, 16)
POOL_GW = HALF // len(POOL_WINDOWS)
POOL_HALO = 16
BLK = 256
CHUNK = 4
MOBA_TOPK = 3
DSA_TOPK_MAX = 256
IDX_HEADS = 8
IDX_DIM = 64
CONV_WIDTH = 31
CONV_HALO = 32
SUBLANES = 8
EPS = 1e-6
NEG = -1e30
LOG2E = 1.4426950408889634
INT_MIN = -(2 ** 31)
I16_MIN = -(2 ** 15)
NEG_BITS = int(np.array(NEG, np.float32).view(np.int32))
V_ROWS = HEAD_DIM + 16

EV_N = 4 * HALF + D_MODEL
OD_N = 8192
OD_QI = 7 * HALF
OD_KI = OD_QI + IDX_HEADS * IDX_DIM
OD_WI = OD_KI + 128

VMEM_LIMIT = 56 * 1024 * 1024


def _cparams(n_axes, vmem=VMEM_LIMIT):
    return pltpu.CompilerParams(dimension_semantics=("arbitrary",) * n_axes, vmem_limit_bytes=vmem)


def _sigmoid(x):
    return 1.0 / (1.0 + jnp.exp(-x))


def _mods_kernel(c_ref, w_ref, b_ref, o_ref):
    c = c_ref[...]
    ca = c * _sigmoid(c)
    o_ref[0] = jnp.dot(ca, w_ref[0], precision=lax.Precision.HIGHEST,
                       preferred_element_type=F32) + b_ref[0]


def _mods(c, ada_w, ada_b):
    depth, d, n = ada_w.shape
    tn = 768
    c8 = jnp.broadcast_to(c.astype(F32), (8, d))
    return pl.pallas_call(
        _mods_kernel,
        grid=(depth, n // tn),
        in_specs=[pl.BlockSpec((8, d), lambda l, j: (0, 0)),
                  pl.BlockSpec((1, d, tn), lambda l, j: (l, 0, j)),
                  pl.BlockSpec((1, 1, tn), lambda l, j: (l, 0, j))],
        out_specs=pl.BlockSpec((1, 8, tn), lambda l, j: (l, 0, j)),
        out_shape=jax.ShapeDtypeStruct((depth, 8, n), F32),
        compiler_params=_cparams(2),
        name="adaln_mods",
    )(c8, ada_w, ada_b.reshape(depth, 1, n))


def _proj_kernel(x_ref, g_ref, scl_ref, sh_ref, w_ref, gain_ref, o_ref, h_ref, *, norm_tiles, tn):
    j = pl.program_id(1)

    @pl.when(j == 0)
    def _():
        x = x_ref[...]
        y = x * lax.rsqrt(jnp.mean(x * x, axis=-1, keepdims=True) + EPS)
        h = (y * g_ref[...]) * (1.0 + scl_ref[...]) + sh_ref[...]
        h_ref[...] = h.astype(BF16)

    z = jnp.dot(h_ref[...], w_ref[...], preferred_element_type=F32)

    is_qk = jnp.logical_and(j >= norm_tiles[0], j < norm_tiles[1])

    @pl.when(is_qk)
    def _():
        for c in range(tn // HEAD_DIM):
            cs = slice(c * HEAD_DIM, (c + 1) * HEAD_DIM)
            zc = z[:, cs]
            r = lax.rsqrt(jnp.mean(zc * zc, axis=-1, keepdims=True) + EPS)
            o_ref[:, cs] = (zc * r * gain_ref[:, cs]).astype(BF16)

    @pl.when(jnp.logical_not(is_qk))
    def _():
        o_ref[...] = z.astype(BF16)


def _proj(x2, g, scl, sh, w, gain, *, qk_col0):
    s, d = x2.shape
    n = w.shape[1]
    tm = min(1024, s)
    tn = 512
    return pl.pallas_call(
        functools.partial(_proj_kernel, norm_tiles=(qk_col0 // tn, (qk_col0 + 2 * HALF) // tn), tn=tn),
        grid=(s // tm, n // tn),
        in_specs=[pl.BlockSpec((tm, d), lambda i, j: (i, 0)),
                  pl.BlockSpec((1, d), lambda i, j: (0, 0)),
                  pl.BlockSpec((1, d), lambda i, j: (0, 0)),
                  pl.BlockSpec((1, d), lambda i, j: (0, 0)),
                  pl.BlockSpec((d, tn), lambda i, j: (0, j)),
                  pl.BlockSpec((1, tn), lambda i, j: (0, j))],
        out_specs=pl.BlockSpec((tm, tn), lambda i, j: (i, j)),
        out_shape=jax.ShapeDtypeStruct((s, n), BF16),
        scratch_shapes=[pltpu.VMEM((tm, d), BF16)],
        compiler_params=_cparams(2),
        name="norm_proj",
    )(x2, g, scl, sh, w, gain)


def _out_kernel(ya_ref, yb_ref, ga_ref, gb_ref, x_ref, gr_ref, w_ref, o_ref, a_ref):
    j = pl.program_id(1)

    @pl.when(j == 0)
    def _():
        for y_ref, g_ref, cs in ((ya_ref, ga_ref, slice(0, HALF)), (yb_ref, gb_ref, slice(HALF, 2 * HALF))):
            g = g_ref[...].astype(F32)
            a_ref[:, cs] = (y_ref[...].astype(F32) * (g * _sigmoid(g))).astype(BF16)

    acc = jnp.dot(a_ref[...], w_ref[...], preferred_element_type=F32)
    o_ref[...] = x_ref[...] + gr_ref[...] * acc


def _outproj(ya, yb, z, g_col_blk, x2, gate_res, w):
    s, d = x2.shape
    tm = min(1024, s)
    tn = 512
    return pl.pallas_call(
        _out_kernel,
        grid=(s // tm, d // tn),
        in_specs=[pl.BlockSpec((tm, HALF), lambda i, j: (i, 0)),
                  pl.BlockSpec((tm, HALF), lambda i, j: (i, 0)),
                  pl.BlockSpec((tm, HALF), lambda i, j: (i, g_col_blk)),
                  pl.BlockSpec((tm, HALF), lambda i, j: (i, g_col_blk + 1)),
                  pl.BlockSpec((tm, tn), lambda i, j: (i, j)),
                  pl.BlockSpec((1, tn), lambda i, j: (0, j)),
                  pl.BlockSpec((2 * HALF, tn), lambda i, j: (0, j))],
        out_specs=pl.BlockSpec((tm, tn), lambda i, j: (i, j)),
        out_shape=jax.ShapeDtypeStruct((s, d), F32),
        scratch_shapes=[pltpu.VMEM((tm, 2 * HALF), BF16)],
        compiler_params=_cparams(2),
        name="gated_outproj",
    )(ya, yb, z, z, x2, gate_res, w)


def _pool_kernel(u_ref, halo_ref, w_ref, ps_ref, o_ref, ext_ref, *, tm):
    i = pl.program_id(0)
    halo = halo_ref[...].astype(F32)
    ext_ref[0:POOL_HALO, :] = jnp.where(i > 0, halo, 0.0)
    ext_ref[POOL_HALO:POOL_HALO + tm, :] = u_ref[...].astype(F32)
    t1 = (i * tm + 1 + lax.broadcasted_iota(I32, (tm, 1), 0)).astype(F32)
    for g, win in enumerate(POOL_WINDOWS):
        cs = slice(g * POOL_GW, (g + 1) * POOL_GW)
        u = ext_ref[POOL_HALO:POOL_HALO + tm, cs]
        acc = u
        for j in range(1, win):
            acc = acc + ext_ref[POOL_HALO - j:POOL_HALO - j + tm, cs]
        dlt = acc / jnp.minimum(t1, float(win)) - u
        y = jnp.dot(dlt.astype(BF16), w_ref[g], preferred_element_type=F32)
        o_ref[:, cs] = (y * ps_ref[:, cs]).astype(BF16)


def _pool(z, u_col_blk, pool_w, pool_scale):
    s = z.shape[0]
    tm = min(512, s)
    hb = tm // POOL_HALO
    return pl.pallas_call(
        functools.partial(_pool_kernel, tm=tm),
        grid=(s // tm,),
        in_specs=[pl.BlockSpec((tm, HALF), lambda i: (i, u_col_blk)),
                  pl.BlockSpec((POOL_HALO, HALF),
                               lambda i: (jnp.maximum(i * hb - 1, 0), u_col_blk)),
                  pl.BlockSpec((len(POOL_WINDOWS), POOL_GW, POOL_GW), lambda i: (0, 0, 0)),
                  pl.BlockSpec((1, HALF), lambda i: (0, 0))],
        out_specs=pl.BlockSpec((tm, HALF), lambda i: (i, 0)),
        out_shape=jax.ShapeDtypeStruct((s, HALF), BF16),
        scratch_shapes=[pltpu.VMEM((POOL_HALO + tm, HALF), F32)],
        compiler_params=_cparams(1),
        name="pool_mixer",
    )(z, z, pool_w, pool_scale)


def _conv_kernel(a_ref, b_ref, ah_ref, bh_ref, cw_ref, cb_ref, lg_ref, lb_ref, o_ref,
                 ext_ref, sh_ref, y_ref, *, tm):
    i = pl.program_id(0)

    def glu(a, b):
        return a.astype(F32) * _sigmoid(b.astype(F32))

    ext_ref[0:CONV_HALO, :] = jnp.where(i > 0, glu(ah_ref[...], bh_ref[...]), 0.0)
    ext_ref[CONV_HALO:CONV_HALO + tm, :] = glu(a_ref[...], b_ref[...])
    span = CONV_HALO + tm - SUBLANES
    for r in range(1, SUBLANES):
        sh_ref[r - 1, 0:span, :] = ext_ref[r:r + span, :]
    off = CONV_HALO - (CONV_WIDTH - 1)
    for ct in range(HALF // 128):
        cs = slice(ct * 128, (ct + 1) * 128)
        acc = None
        for j in range(CONV_WIDTH):
            r, base = (off + j) % SUBLANES, (off + j) // SUBLANES * SUBLANES
            src = ext_ref[base:base + tm, cs] if r == 0 else sh_ref[r - 1, base:base + tm, cs]
            term = cw_ref[j:j + 1, cs] * src
            acc = term if acc is None else acc + term
        y_ref[:, cs] = acc + cb_ref[:, cs]
    y = y_ref[...]
    mu = jnp.mean(y, axis=-1, keepdims=True)
    yc = y - mu
    var = jnp.mean(yc * yc, axis=-1, keepdims=True)
    yn = yc * lax.rsqrt(var + EPS) * lg_ref[...] + lb_ref[...]
    o_ref[...] = (yn * _sigmoid(yn)).astype(BF16)


def _conv(z, a_col_blk, b_col_blk, conv_w, conv_b, ln_g, ln_b):
    s = z.shape[0]
    tm = min(256, s)
    hb = tm // CONV_HALO
    cwp = jnp.zeros((32, HALF), F32).at[:CONV_WIDTH].set(conv_w)
    return pl.pallas_call(
        functools.partial(_conv_kernel, tm=tm),
        grid=(s // tm,),
        in_specs=[pl.BlockSpec((tm, HALF), lambda i: (i, a_col_blk)),
                  pl.BlockSpec((tm, HALF), lambda i: (i, b_col_blk)),
                  pl.BlockSpec((CONV_HALO, HALF), lambda i: (jnp.maximum(i * hb - 1, 0), a_col_blk)),
                  pl.BlockSpec((CONV_HALO, HALF), lambda i: (jnp.maximum(i * hb - 1, 0), b_col_blk)),
                  pl.BlockSpec((32, HALF), lambda i: (0, 0)),
                  pl.BlockSpec((1, HALF), lambda i: (0, 0)),
                  pl.BlockSpec((1, HALF), lambda i: (0, 0)),
                  pl.BlockSpec((1, HALF), lambda i: (0, 0))],
        out_specs=pl.BlockSpec((tm, HALF), lambda i: (i, 0)),
        out_shape=jax.ShapeDtypeStruct((s, HALF), BF16),
        scratch_shapes=[pltpu.VMEM((CONV_HALO + tm, HALF), F32),
                        pltpu.VMEM((SUBLANES - 1, CONV_HALO + tm, HALF), F32),
                        pltpu.VMEM((tm, HALF), F32)],
        compiler_params=_cparams(1),
        name="conv_module",
    )(z, z, z, z, cwp, conv_b, ln_g, ln_b)


def _alibi_operands(slopes):
    s2 = slopes.astype(F32) * LOG2E
    pieces = []
    rest = s2
    for _ in range(3):
        part = rest.astype(BF16).astype(F32)
        pieces.append(part)
        rest = rest - part
    pieces = jnp.broadcast_to(jnp.stack(pieces, axis=-1)[:, None, :], (N_HEADS, BLK, 3))
    pos = jnp.broadcast_to(jnp.arange(BLK, dtype=F32)[None, :, None], (N_HEADS, BLK, 3))
    pad = jnp.zeros((N_HEADS, BLK, HEAD_DIM - 6), F32)
    e_k = jnp.concatenate([pos, pieces, pad], axis=-1).astype(BF16)
    e_q = jnp.concatenate([pieces, -pos, pad], axis=-1).astype(BF16)
    return e_k, e_q


def _causal_tile():
    jj = lax.broadcasted_iota(I32, (BLK, BLK), 0)
    ii = lax.broadcasted_iota(I32, (BLK, BLK), 1)
    return jj <= ii


def _pipe_step(ops, m, acc, x=None, y=None, z=None):
    meta, score_tile, value_tile, valid = ops
    if x is not None:
        ix, s_x = x
        state_x, c2_x, flag_x, _ = meta(ix)
    if y is not None:
        iy, s_y, p_y, bm_y = y
        _, c2_y, flag_y, reset = meta(iy)
        m_old = m if reset is None else jnp.where(reset, NEG, m)
        m = jnp.maximum(m_old, bm_y)
        alpha_y = jnp.exp2(m_old - m)
    if z is not None:
        iz, p_z, alpha_z = z
        state_z, _, _, _ = meta(iz)
    bm_x = None
    pv = None
    for b in range(CHUNK):
        rows = slice(b * BLK, (b + 1) * BLK)
        if x is not None:
            t = score_tile(state_x, b)
            s_x[rows, :] = t
            bm = jnp.max(t, axis=0, keepdims=True) + c2_x[b]
            if flag_x[b] is not None:
                bm = jnp.where(flag_x[b], bm, NEG)
            bm_x = bm if bm_x is None else jnp.maximum(bm_x, bm)
        if y is not None:
            sub = m - c2_y[b]
            if flag_y[b] is not None:
                sub = jnp.where(flag_y[b], sub, -NEG)
            p_y[rows, :] = jnp.exp2(s_y[rows, :] - sub).astype(BF16)
        if z is not None:
            d = jnp.dot(value_tile(state_z, b), p_z[rows, :], preferred_element_type=F32)
            pv = d if pv is None else pv + d
    out = {}
    if x is not None:
        out["bm"] = jnp.where(valid(ix), bm_x, NEG)
    if y is not None:
        out["alpha"] = alpha_y
    if z is not None:
        acc = alpha_z * acc + pv
    return m, acc, out


def _load_state(state_ref):
    return tuple(state_ref[k, 0:1, :] for k in range(4))


def _store_state(state_ref, m, bm1, bm2, alpha):
    for k, v in enumerate((m, bm1, bm2, alpha)):
        state_ref[k] = jnp.broadcast_to(v, (SUBLANES, BLK))


def _pipe_trip(ops, slots, first_item, acc, emit=None):
    s_slots, p_slots, state_ref = slots[:4], slots[4:6], slots[6]
    m, bm1, bm2, alpha = _load_state(state_ref)
    for u in range(4):
        i = first_item + u
        m, acc, o = _pipe_step(ops, m, acc,
                               x=(i + 3, s_slots[(u + 3) % 4]),
                               y=(i + 1, s_slots[(u + 1) % 4], p_slots[(u + 1) % 2], bm1),
                               z=(i, p_slots[u % 2], alpha))
        if emit is not None:
            emit(i, acc)
        bm1, bm2, alpha = bm2, o["bm"], o["alpha"]
    _store_state(state_ref, m, bm1, bm2, alpha)
    return acc


def _pipe_fill(ops, slots):
    s_slots, p_slots, state_ref = slots[:4], slots[4:6], slots[6]
    m = jnp.full((1, BLK), NEG, F32)
    acc = jnp.zeros((V_ROWS, BLK), F32)
    m, acc, o0 = _pipe_step(ops, m, acc, x=(0, s_slots[0]))
    m, acc, o1 = _pipe_step(ops, m, acc, x=(1, s_slots[1]))
    m, acc, o2 = _pipe_step(ops, m, acc, x=(2, s_slots[2]), y=(0, s_slots[0], p_slots[0], o0["bm"]))
    _store_state(state_ref, m, o1["bm"], o2["bm"], o2["alpha"])
    return acc


def _attend(n, ops, slots):
    s_slots, p_slots, state_ref = slots[:4], slots[4:6], slots[6]
    acc = lax.fori_loop(0, n // 4, lambda j, a: _pipe_trip(ops, slots, 4 * j, a), _pipe_fill(ops, slots))
    base = (n // 4) * 4
    for u in range(3):
        def tail(acc=acc, u=u):
            m, bm1, bm2, alpha = _load_state(state_ref)
            m, acc_new, o = _pipe_step(ops, m, acc,
                                       y=(base + u + 1, s_slots[(u + 1) % 4], p_slots[(u + 1) % 2], bm1),
                                       z=(base + u, p_slots[u % 2], alpha))
            _store_state(state_ref, m, bm2, jnp.full((1, BLK), NEG, F32), o["alpha"])
            return acc_new
        acc = lax.cond(n % 4 > u, tail, lambda acc=acc: acc)
    return acc[0:HEAD_DIM] / acc[HEAD_DIM:HEAD_DIM + 1]


def _pipeline_slots():
    return ([pltpu.VMEM((CHUNK * BLK, BLK), F32)] * 4 + [pltpu.VMEM((CHUNK * BLK, BLK), BF16)] * 2
            + [pltpu.VMEM((4, SUBLANES, BLK), F32)])


def _scores_t(k_blk, q):
    return lax.dot_general(k_blk, q, (((1,), (1,)), ((), ())), preferred_element_type=F32)


def _moba_kernel(slopes_ref, q_ref, k_ref, vt_ref, ek_ref, eq_ref, o_ref, km_ref, sel_ref,
                 *slot_refs, nb, n_sel):
    h = pl.program_id(0)
    qi = pl.program_id(1)
    nbp = sel_ref.shape[0]

    @pl.when(qi == 0)
    def _():
        km_ref[...] = jnp.zeros_like(km_ref)

        def mean_body(b, carry):
            blk = k_ref[pl.ds(pl.multiple_of(b * BLK, BLK), BLK), :].astype(F32)
            km_ref[pl.ds(b, 1), :] = jnp.sum(blk, axis=0, keepdims=True) * (1.0 / BLK)
            return carry

        lax.fori_loop(0, nb, mean_body, 0)

    slope2 = slopes_ref[h] * LOG2E
    causal = _causal_tile()
    q = q_ref[...]
    q_ext = jnp.concatenate([q, eq_ref[0]], axis=1)
    e_k = ek_ref[0]

    g_t = lax.dot_general(km_ref[...], q.astype(F32), (((1,), (1,)), ((), ())),
                          precision=lax.Precision.HIGHEST, preferred_element_type=F32)
    row = lax.broadcasted_iota(I32, (nbp, BLK), 0)
    past = row < qi
    cur = jnp.where(past, g_t, NEG)
    picked = jnp.zeros((nbp, BLK), jnp.bool_)
    for _ in range(n_sel):
        mx = jnp.max(cur, axis=0, keepdims=True)
        first = jnp.min(jnp.where(cur == mx, row, nbp), axis=0, keepdims=True)
        hit = row == first
        picked = picked | hit
        cur = jnp.where(hit, -3e38, cur)
    sel_ref[...] = jnp.where(picked & past, 1.0, 0.0)

    n_main = qi // CHUNK

    def meta(c):
        c = jnp.minimum(c, n_main)
        tail = c == n_main
        kbs, c2s, flags = [], [], []
        for b in range(CHUNK):
            raw = jnp.where(tail, qi - (CHUNK - 1) + b, c * CHUNK + b)
            kb = jnp.clip(raw, 0, nb - 1)
            enabled = jnp.logical_or(jnp.logical_not(tail), raw >= n_main * CHUNK)
            sel_row = jnp.where(enabled, sel_ref[pl.ds(kb, 1), :], 0.0)
            if b == CHUNK - 1:
                sel_row = jnp.where(tail, 1.0, sel_row)
            kbs.append(kb)
            c2s.append((qi - kb).astype(F32) * (-slope2 * BLK))
            flags.append(sel_row > 0.5)
        return (tail, kbs), c2s, flags, None

    def score_tile(state, b):
        tail, kbs = state
        k_blk = k_ref[pl.ds(pl.multiple_of(kbs[b] * BLK, BLK), BLK), :]
        t = _scores_t(jnp.concatenate([k_blk, e_k], axis=1), q_ext)
        if b == CHUNK - 1:
            t = jnp.where(tail, jnp.where(causal, t, NEG), t)
        return t

    def value_tile(state, b):
        return vt_ref[0, state[1][b]]

    out_t = _attend(n_main + 1, (meta, score_tile, value_tile, lambda c: c <= n_main), slot_refs)
    o_ref[...] = jnp.transpose(out_t).astype(BF16)


def _moba(z, q_col_blk0, k_col_blk0, v_t, slopes, e_k, e_q):
    s = z.shape[0]
    nb = s // BLK
    assert nb % CHUNK == 0
    nbp = -(-nb // 8) * 8
    n_sel = max(1, min(MOBA_TOPK, nb - 1))
    grid_spec = pltpu.PrefetchScalarGridSpec(
        num_scalar_prefetch=1,
        grid=(N_HEADS, nb),
        in_specs=[pl.BlockSpec((BLK, HEAD_DIM), lambda h, i, sl: (i, q_col_blk0 + h)),
                  pl.BlockSpec((s, HEAD_DIM), lambda h, i, sl: (0, k_col_blk0 + h)),
                  pl.BlockSpec((1, nb, V_ROWS, BLK), lambda h, i, sl: (h, 0, 0, 0)),
                  pl.BlockSpec((1, BLK, HEAD_DIM), lambda h, i, sl: (h, 0, 0)),
                  pl.BlockSpec((1, BLK, HEAD_DIM), lambda h, i, sl: (h, 0, 0))],
        out_specs=pl.BlockSpec((BLK, HEAD_DIM), lambda h, i, sl: (i, h)),
        scratch_shapes=[pltpu.VMEM((nbp, HEAD_DIM), F32),
                        pltpu.VMEM((nbp, BLK), F32)] + _pipeline_slots(),
    )
    return pl.pallas_call(
        functools.partial(_moba_kernel, nb=nb, n_sel=n_sel),
        grid_spec=grid_spec,
        out_shape=jax.ShapeDtypeStruct((s, HALF), BF16),
        compiler_params=_cparams(2),
        name="moba_attention",
    )(slopes, z, z, v_t, e_k, e_q)


def _sortable(x):
    b = pltpu.bitcast(x, I32)
    return jnp.where(b < 0, b ^ 0x7FFFFFFF, b)


def _dsa_kernel(slopes_ref, qi_ref, wi_ref, ki_ref, q_ref, k_ref, vt_ref, ek_ref, eq_ref, o_ref,
                key_ref, half_ref, qm_ref, *slot_refs, n_sel):
    index_bits = max(1, (key_ref.shape[0] - 1).bit_length())
    qt = pl.program_id(0)
    h = pl.program_id(1)
    causal = _causal_tile()
    n_chunks = (qt + CHUNK) // CHUNK

    @pl.when(h == 0)
    def _():
        w_t = jnp.transpose(wi_ref[...].astype(F32))[0:IDX_HEADS, :]
        w_t = w_t * (IDX_HEADS ** -0.5 * IDX_DIM ** -0.5)
        lane = lax.broadcasted_iota(I32, (BLK, 128), 1)
        for hp in range(IDX_HEADS // 2):
            pair = qi_ref[:, hp * 128:(hp + 1) * 128]
            qm_ref[2 * hp] = jnp.where(lane < IDX_DIM, pair, jnp.zeros_like(pair))
            qm_ref[2 * hp + 1] = jnp.where(lane >= IDX_DIM, pair, jnp.zeros_like(pair))

        def score_body(c, carry):
            rows_c = pl.ds(pl.multiple_of(c * (CHUNK * BLK), CHUNK * BLK), CHUNK * BLK)
            ki_blk = ki_ref[rows_c, :]
            sc = jnp.zeros((CHUNK * BLK, BLK), F32)
            for hh in range(IDX_HEADS):
                lg = _scores_t(ki_blk, qm_ref[hh])
                sc = sc + w_t[hh:hh + 1, :] * jnp.maximum(lg, 0.0)
            sk = _sortable(sc)
            for b in range(CHUNK):
                kb = c * CHUNK + b
                blk = sk[b * BLK:(b + 1) * BLK]
                on_diag = jnp.where(causal, blk, INT_MIN)
                keys = jnp.where(kb < qt, blk, jnp.where(kb == qt, on_diag, INT_MIN))
                rows = pl.ds(pl.multiple_of(kb * BLK, BLK), BLK)
                key_ref[rows, :] = keys
                half_ref[rows, :] = lax.shift_right_arithmetic(keys, 16).astype(I16)
            return carry

        lax.fori_loop(0, n_chunks, score_body, 0)

        def count_ge16(cand):
            def cbody(c, cnts):
                chunk = half_ref.at[pl.ds(pl.multiple_of(c * (CHUNK * BLK), CHUNK * BLK), CHUNK * BLK), :]
                cnts = list(cnts)
                for r in range(CHUNK * BLK // 16):
                    hit = jnp.where(chunk[r * 16:(r + 1) * 16, :] >= cand, jnp.int16(1), jnp.int16(0))
                    cnts[r % len(cnts)] = cnts[r % len(cnts)] + hit
                return tuple(cnts)
            cnts = lax.fori_loop(0, n_chunks, cbody, tuple(jnp.zeros((16, BLK), I16) for _ in range(4)))
            tot = cnts[0].astype(I32) + cnts[1].astype(I32) + cnts[2].astype(I32) + cnts[3].astype(I32)
            return jnp.sum(tot, axis=0, keepdims=True)

        def bisect16(need):
            def bit_body(b, tb):
                cand_b = tb | lax.shift_left(jnp.int32(1), 15 - b)
                cnt = count_ge16((cand_b + I16_MIN).astype(I16))
                return jnp.where(cnt >= need, cand_b, tb)
            return lax.fori_loop(0, 16, bit_body, jnp.zeros((1, BLK), I32))

        t_hi = bisect16(n_sel) + I16_MIN
        above = count_ge16(jnp.minimum(t_hi + 1, -I16_MIN - 1).astype(I16))
        above = jnp.where(t_hi == -I16_MIN - 1, 0, above)

        def low_body(c, carry):
            rows = pl.ds(pl.multiple_of(c * (CHUNK * BLK), CHUNK * BLK), CHUNK * BLK)
            keys = key_ref[rows, :]
            low = (keys & 0xFFFF) + I16_MIN
            same = lax.shift_right_arithmetic(keys, 16) == t_hi
            half_ref[rows, :] = jnp.where(same, low, I16_MIN).astype(I16)
            return carry

        lax.fori_loop(0, n_chunks, low_body, 0)
        t_lo = bisect16(n_sel - above)
        thr = jnp.maximum(lax.shift_left(t_hi, 16) + t_lo, INT_MIN + 1)

        def count32(pred):
            def cbody(c, cnt):
                base = c * (CHUNK * BLK)
                keys = key_ref[pl.ds(pl.multiple_of(base, CHUNK * BLK), CHUNK * BLK), :]
                idx = base + lax.broadcasted_iota(I32, (CHUNK * BLK, BLK), 0)
                hit = jnp.where(pred(keys, idx), 1, 0).reshape(CHUNK * BLK // SUBLANES, SUBLANES, BLK)
                return cnt + jnp.sum(hit, axis=0)
            cnt = lax.fori_loop(0, n_chunks, cbody, jnp.zeros((SUBLANES, BLK), I32))
            return jnp.sum(cnt, axis=0, keepdims=True)

        excess = count32(lambda keys, idx: keys >= thr) - n_sel
        n_keys = n_chunks * (CHUNK * BLK)

        def tie_limit():
            keep = count32(lambda keys, idx: keys == thr) - excess
            def bit_body(b, j):
                cand = j | lax.shift_left(jnp.int32(1), index_bits - 1 - b)
                below = count32(lambda keys, idx: (keys == thr) & (idx < cand))
                return jnp.where(below <= keep - 1, cand, j)
            j = lax.fori_loop(0, index_bits, bit_body, jnp.zeros((1, BLK), I32))
            return jnp.where(excess > 0, j, n_keys)

        last_tied = lax.cond(jnp.max(excess) > 0, tie_limit, lambda: jnp.full((1, BLK), n_keys, I32))

        def mask_body(c, carry):
            base = c * (CHUNK * BLK)
            rows = pl.ds(pl.multiple_of(base, CHUNK * BLK), CHUNK * BLK)
            keys = key_ref[rows, :]
            idx = base + lax.broadcasted_iota(I32, (CHUNK * BLK, BLK), 0)
            chosen = (keys > thr) | ((keys == thr) & (idx <= last_tied))
            key_ref[rows, :] = jnp.where(chosen, 0, NEG_BITS)
            return carry

        lax.fori_loop(0, n_chunks, mask_body, 0)

    slope2 = slopes_ref[h] * LOG2E
    q_ext = jnp.concatenate([q_ref[...], eq_ref[0]], axis=1)
    e_k = ek_ref[0]

    no_flags = [None] * CHUNK

    def chunk_meta(c):
        kbs = [jnp.minimum(c, n_chunks - 1) * CHUNK + b for b in range(CHUNK)]
        return kbs, [jnp.maximum(qt - kb, 0).astype(F32) * (-slope2 * BLK) for kb in kbs], no_flags, None

    def score_tile(kbs, b):
        rows = pl.ds(pl.multiple_of(kbs[b] * BLK, BLK), BLK)
        k_ext = jnp.concatenate([k_ref[rows, :], e_k], axis=1)
        return _scores_t(k_ext, q_ext) + pltpu.bitcast(key_ref[rows, :], F32)

    def value_tile(kbs, b):
        return vt_ref[0, kbs[b]]

    out_t = _attend(n_chunks, (chunk_meta, score_tile, value_tile, lambda c: c < n_chunks), slot_refs)
    o_ref[...] = jnp.transpose(out_t).astype(BF16)


def _dsa(z, v_t, slopes, e_k, e_q):
    s = z.shape[0]
    nb = s // BLK
    assert nb % CHUNK == 0
    n_sel = min(DSA_TOPK_MAX, s // 4)
    grid_spec = pltpu.PrefetchScalarGridSpec(
        num_scalar_prefetch=1,
        grid=(nb, N_HEADS),
        in_specs=[pl.BlockSpec((BLK, IDX_HEADS * IDX_DIM), lambda i, h, sl: (i, OD_QI // (IDX_HEADS * IDX_DIM))),
                  pl.BlockSpec((BLK, 128), lambda i, h, sl: (i, OD_WI // 128)),
                  pl.BlockSpec((s, 128), lambda i, h, sl: (0, OD_KI // 128)),
                  pl.BlockSpec((BLK, HEAD_DIM), lambda i, h, sl: (i, h)),
                  pl.BlockSpec((s, HEAD_DIM), lambda i, h, sl: (0, N_HEADS + h)),
                  pl.BlockSpec((1, nb, V_ROWS, BLK), lambda i, h, sl: (h, 0, 0, 0)),
                  pl.BlockSpec((1, BLK, HEAD_DIM), lambda i, h, sl: (h, 0, 0)),
                  pl.BlockSpec((1, BLK, HEAD_DIM), lambda i, h, sl: (h, 0, 0))],
        out_specs=pl.BlockSpec((BLK, HEAD_DIM), lambda i, h, sl: (i, h)),
        scratch_shapes=[pltpu.VMEM((s, BLK), I32),
                        pltpu.VMEM((s, BLK), I16),
                        pltpu.VMEM((IDX_HEADS, BLK, 128), BF16)] + _pipeline_slots(),
    )
    return pl.pallas_call(
        functools.partial(_dsa_kernel, n_sel=n_sel),
        grid_spec=grid_spec,
        out_shape=jax.ShapeDtypeStruct((s, HALF), BF16),
        compiler_params=_cparams(2),
        name="dsa_attention",
    )(slopes, z, z, z, z, z, v_t, e_k, e_q)


def _v_transposed(z, col0):
    s = z.shape[0]
    nb = s // BLK
    v = z[:, col0:col0 + HALF].reshape(nb, BLK, N_HEADS, HEAD_DIM)
    v_t = jnp.transpose(v, (2, 0, 3, 1))
    ones = jnp.ones((N_HEADS, nb, 8, BLK), v_t.dtype)
    return jnp.concatenate([v_t, ones, jnp.zeros((N_HEADS, nb, V_ROWS - HEAD_DIM - 8, BLK), v_t.dtype)], axis=2)


def _qk_gain(qn, kn, col0, n):
    gq = jnp.tile(qn.astype(F32), N_HEADS) * (HEAD_DIM ** -0.5 * LOG2E)
    gk = jnp.tile(kn.astype(F32), N_HEADS)
    return jnp.concatenate([jnp.ones((col0,), F32), gq, gk, jnp.ones((n - col0 - 2 * HALF,), F32)])[None, :]


def kernel(x, c, norm_g, ada_w, ada_b, w_out, ev_w_in, pool_w, pool_scale, moba_qn, moba_kn,
           od_w_in, dsa_qn, dsa_kn, conv_w, conv_b, conv_ln_g, conv_ln_b):
    b, s, d = x.shape
    assert b == 1 and d == D_MODEL and s % BLK == 0
    depth = norm_g.shape[0]
    x2 = x[0].astype(F32)
    mods = _mods(c, ada_w, ada_b)[:, 0:1, :]
    slopes = jnp.exp2(-8.0 * jnp.arange(1, N_HEADS + 1, dtype=F32) / N_HEADS)
    e_k, e_q = _alibi_operands(slopes)

    for l in range(depth):
        shift, scl, gate_res = (mods[l, :, i * d:(i + 1) * d] for i in range(3))
        g = norm_g[l][None, :].astype(F32)
        if l % 2 == 0:
            e = l // 2
            w = ev_w_in[e].astype(BF16)
            z = _proj(x2, g, scl, shift, w, _qk_gain(moba_qn[e], moba_kn[e], HALF, EV_N), qk_col0=HALF)
            ya = _pool(z, 0, pool_w[e].astype(BF16), pool_scale[e][None, :].astype(F32))
            yb = _moba(z, N_HEADS, 2 * N_HEADS, _v_transposed(z, 3 * HALF), slopes, e_k, e_q)
            g_blk = 4
        else:
            o = l // 2
            w = od_w_in[o]
            c_qi = 3 * HALF
            c_wi = c_qi + IDX_HEADS * IDX_DIM
            c_ki = c_wi + IDX_HEADS
            c_ga = c_ki + IDX_DIM
            ki_w = w[:, c_ki:c_ga]
            w = jnp.concatenate([
                w[:, 0:3 * HALF], w[:, c_ga:],
                w[:, c_qi:c_wi], ki_w, ki_w,
                w[:, c_wi:c_ki], jnp.zeros((d, OD_N - OD_WI - IDX_HEADS), w.dtype)], axis=1).astype(BF16)
            z = _proj(x2, g, scl, shift, w, _qk_gain(dsa_qn[o], dsa_kn[o], 0, OD_N), qk_col0=0)
            ya = _dsa(z, _v_transposed(z, 2 * HALF), slopes, e_k, e_q)
            yb = _conv(z, 3, 4, conv_w[o].astype(F32), conv_b[o][None, :].astype(F32),
                       conv_ln_g[o][None, :].astype(F32), conv_ln_b[o][None, :].astype(F32))
            g_blk = 5
        x2 = _outproj(ya, yb, z, g_blk, x2, gate_res, w_out[l].astype(BF16))
    return x2[None].astype(x.dtype)
```

```python
import functools

import jax
import numpy as np
import jax.numpy as jnp
from jax import lax
from jax.experimental import pallas as pl
from jax.experimental.pallas import tpu as pltpu

F32 = jnp.float32
BF16 = jnp.bfloat16
I32 = jnp.int32
I16 = jnp.int16

D_MODEL = 2048
HALF = D_MODEL // 2
HEAD_DIM = 128
N_HEADS = HALF // HEAD_DIM
POOL_WINDOWS = (2, 4, 8, 16)
POOL_GW = HALF // len(POOL_WINDOWS)
POOL_HALO = 16
BLK = 256
CHUNK = 4
MOBA_TOPK = 3
DSA_TOPK_MAX = 256
IDX_HEADS = 8
IDX_DIM = 64
CONV_WIDTH = 31
CONV_HALO = 32
SUBLANES = 8
EPS = 1e-6
NEG = -1e30
LOG2E = 1.4426950408889634
INT_MIN = -(2 ** 31)
I16_MIN = -(2 ** 15)
I16_MAX = 2 ** 15 - 1
NEG_BITS = int(np.array(NEG, np.float32).view(np.int32))
V_ROWS = HEAD_DIM + 16

EV_N = 4 * HALF + D_MODEL
OD_N = 8192
OD_QI = 7 * HALF
OD_KI = OD_QI + IDX_HEADS * IDX_DIM
OD_WI = OD_KI + 128

VMEM_LIMIT = 56 * 1024 * 1024


def _cparams(n_axes, vmem=VMEM_LIMIT):
    return pltpu.CompilerParams(dimension_semantics=("arbitrary",) * n_axes, vmem_limit_bytes=vmem)


def _sigmoid(x):
    return 1.0 / (1.0 + jnp.exp(-x))


def _mods_kernel(c_ref, w_ref, b_ref, o_ref):
    c = c_ref[...]
    ca = c * _sigmoid(c)
    o_ref[0] = jnp.dot(ca, w_ref[0], precision=lax.Precision.HIGHEST,
                       preferred_element_type=F32) + b_ref[0]


def _mods(c, ada_w, ada_b):
    depth, d, n = ada_w.shape
    tn = 768
    c8 = jnp.broadcast_to(c.astype(F32), (8, d))
    return pl.pallas_call(
        _mods_kernel,
        grid=(depth, n // tn),
        in_specs=[pl.BlockSpec((8, d), lambda l, j: (0, 0)),
                  pl.BlockSpec((1, d, tn), lambda l, j: (l, 0, j)),
                  pl.BlockSpec((1, 1, tn), lambda l, j: (l, 0, j))],
        out_specs=pl.BlockSpec((1, 8, tn), lambda l, j: (l, 0, j)),
        out_shape=jax.ShapeDtypeStruct((depth, 8, n), F32),
        compiler_params=_cparams(2),
        name="adaln_mods",
    )(c8, ada_w, ada_b.reshape(depth, 1, n))


def _proj_kernel(x_ref, g_ref, scl_ref, sh_ref, w_ref, gain_ref, o_ref, h_ref, *, norm_tiles, tn):
    j = pl.program_id(1)

    @pl.when(j == 0)
    def _():
        x = x_ref[...]
        y = x * lax.rsqrt(jnp.mean(x * x, axis=-1, keepdims=True) + EPS)
        h = (y * g_ref[...]) * (1.0 + scl_ref[...]) + sh_ref[...]
        h_ref[...] = h.astype(BF16)

    z = jnp.dot(h_ref[...], w_ref[...], preferred_element_type=F32)

    is_qk = jnp.logical_and(j >= norm_tiles[0], j < norm_tiles[1])

    @pl.when(is_qk)
    def _():
        for c in range(tn // HEAD_DIM):
            cs = slice(c * HEAD_DIM, (c + 1) * HEAD_DIM)
            zc = z[:, cs]
            r = lax.rsqrt(jnp.mean(zc * zc, axis=-1, keepdims=True) + EPS)
            o_ref[:, cs] = (zc * r * gain_ref[:, cs]).astype(BF16)

    @pl.when(jnp.logical_not(is_qk))
    def _():
        o_ref[...] = z.astype(BF16)


def _proj(x2, g, scl, sh, w, gain, *, qk_col0):
    s, d = x2.shape
    n = w.shape[1]
    tm = min(1024, s)
    tn = 512
    return pl.pallas_call(
        functools.partial(_proj_kernel, norm_tiles=(qk_col0 // tn, (qk_col0 + 2 * HALF) // tn), tn=tn),
        grid=(s // tm, n // tn),
        in_specs=[pl.BlockSpec((tm, d), lambda i, j: (i, 0)),
                  pl.BlockSpec((1, d), lambda i, j: (0, 0)),
                  pl.BlockSpec((1, d), lambda i, j: (0, 0)),
                  pl.BlockSpec((1, d), lambda i, j: (0, 0)),
                  pl.BlockSpec((d, tn), lambda i, j: (0, j)),
                  pl.BlockSpec((1, tn), lambda i, j: (0, j))],
        out_specs=pl.BlockSpec((tm, tn), lambda i, j: (i, j)),
        out_shape=jax.ShapeDtypeStruct((s, n), BF16),
        scratch_shapes=[pltpu.VMEM((tm, d), BF16)],
        compiler_params=_cparams(2),
        name="norm_proj",
    )(x2, g, scl, sh, w, gain)


def _out_kernel(ya_ref, yb_ref, ga_ref, gb_ref, x_ref, gr_ref, w_ref, o_ref, a_ref):
    j = pl.program_id(1)

    @pl.when(j == 0)
    def _():
        for y_ref, g_ref, cs in ((ya_ref, ga_ref, slice(0, HALF)), (yb_ref, gb_ref, slice(HALF, 2 * HALF))):
            g = g_ref[...].astype(F32)
            a_ref[:, cs] = (y_ref[...].astype(F32) * (g * _sigmoid(g))).astype(BF16)

    acc = jnp.dot(a_ref[...], w_ref[...], preferred_element_type=F32)
    o_ref[...] = x_ref[...] + gr_ref[...] * acc


def _outproj(ya, yb, z, g_col_blk, x2, gate_res, w):
    s, d = x2.shape
    tm = min(1024, s)
    tn = 512
    return pl.pallas_call(
        _out_kernel,
        grid=(s // tm, d // tn),
        in_specs=[pl.BlockSpec((tm, HALF), lambda i, j: (i, 0)),
                  pl.BlockSpec((tm, HALF), lambda i, j: (i, 0)),
                  pl.BlockSpec((tm, HALF), lambda i, j: (i, g_col_blk)),
                  pl.BlockSpec((tm, HALF), lambda i, j: (i, g_col_blk + 1)),
                  pl.BlockSpec((tm, tn), lambda i, j: (i, j)),
                  pl.BlockSpec((1, tn), lambda i, j: (0, j)),
                  pl.BlockSpec((2 * HALF, tn), lambda i, j: (0, j))],
        out_specs=pl.BlockSpec((tm, tn), lambda i, j: (i, j)),
        out_shape=jax.ShapeDtypeStruct((s, d), F32),
        scratch_shapes=[pltpu.VMEM((tm, 2 * HALF), BF16)],
        compiler_params=_cparams(2),
        name="gated_outproj",
    )(ya, yb, z, z, x2, gate_res, w)


def _pool_kernel(u_ref, halo_ref, w_ref, ps_ref, o_ref, ext_ref, *, tm):
    i = pl.program_id(0)
    halo = halo_ref[...].astype(F32)
    ext_ref[0:POOL_HALO, :] = jnp.where(i > 0, halo, 0.0)
    ext_ref[POOL_HALO:POOL_HALO + tm, :] = u_ref[...].astype(F32)
    t1 = (i * tm + 1 + lax.broadcasted_iota(I32, (tm, 1), 0)).astype(F32)
    for g, win in enumerate(POOL_WINDOWS):
        cs = slice(g * POOL_GW, (g + 1) * POOL_GW)
        u = ext_ref[POOL_HALO:POOL_HALO + tm, cs]
        acc = u
        for j in range(1, win):
            acc = acc + ext_ref[POOL_HALO - j:POOL_HALO - j + tm, cs]
        dlt = acc / jnp.minimum(t1, float(win)) - u
        y = jnp.dot(dlt.astype(BF16), w_ref[g], preferred_element_type=F32)
        o_ref[:, cs] = (y * ps_ref[:, cs]).astype(BF16)


def _pool(z, u_col_blk, pool_w, pool_scale):
    s = z.shape[0]
    tm = min(512, s)
    hb = tm // POOL_HALO
    return pl.pallas_call(
        functools.partial(_pool_kernel, tm=tm),
        grid=(s // tm,),
        in_specs=[pl.BlockSpec((tm, HALF), lambda i: (i, u_col_blk)),
                  pl.BlockSpec((POOL_HALO, HALF),
                               lambda i: (jnp.maximum(i * hb - 1, 0), u_col_blk)),
                  pl.BlockSpec((len(POOL_WINDOWS), POOL_GW, POOL_GW), lambda i: (0, 0, 0)),
                  pl.BlockSpec((1, HALF), lambda i: (0, 0))],
        out_specs=pl.BlockSpec((tm, HALF), lambda i: (i, 0)),
        out_shape=jax.ShapeDtypeStruct((s, HALF), BF16),
        scratch_shapes=[pltpu.VMEM((POOL_HALO + tm, HALF), F32)],
        compiler_params=_cparams(1),
        name="pool_mixer",
    )(z, z, pool_w, pool_scale)


def _conv_kernel(a_ref, b_ref, ah_ref, bh_ref, cw_ref, cb_ref, lg_ref, lb_ref, o_ref,
                 ext_ref, sh_ref, y_ref, *, tm):
    i = pl.program_id(0)

    def glu(a, b):
        return a.astype(F32) * _sigmoid(b.astype(F32))

    ext_ref[0:CONV_HALO, :] = jnp.where(i > 0, glu(ah_ref[...], bh_ref[...]), 0.0)
    ext_ref[CONV_HALO:CONV_HALO + tm, :] = glu(a_ref[...], b_ref[...])
    span = CONV_HALO + tm - SUBLANES
    for r in range(1, SUBLANES):
        sh_ref[r - 1, 0:span, :] = ext_ref[r:r + span, :]
    off = CONV_HALO - (CONV_WIDTH - 1)
    for ct in range(HALF // 128):
        cs = slice(ct * 128, (ct + 1) * 128)
        acc = None
        for j in range(CONV_WIDTH):
            r, base = (off + j) % SUBLANES, (off + j) // SUBLANES * SUBLANES
            src = ext_ref[base:base + tm, cs] if r == 0 else sh_ref[r - 1, base:base + tm, cs]
            term = cw_ref[j:j + 1, cs] * src
            acc = term if acc is None else acc + term
        y_ref[:, cs] = acc + cb_ref[:, cs]
    y = y_ref[...]
    mu = jnp.mean(y, axis=-1, keepdims=True)
    yc = y - mu
    var = jnp.mean(yc * yc, axis=-1, keepdims=True)
    yn = yc * lax.rsqrt(var + EPS) * lg_ref[...] + lb_ref[...]
    o_ref[...] = (yn * _sigmoid(yn)).astype(BF16)


def _conv(z, a_col_blk, b_col_blk, conv_w, conv_b, ln_g, ln_b):
    s = z.shape[0]
    tm = min(256, s)
    hb = tm // CONV_HALO
    cwp = jnp.zeros((32, HALF), F32).at[:CONV_WIDTH].set(conv_w)
    return pl.pallas_call(
        functools.partial(_conv_kernel, tm=tm),
        grid=(s // tm,),
        in_specs=[pl.BlockSpec((tm, HALF), lambda i: (i, a_col_blk)),
                  pl.BlockSpec((tm, HALF), lambda i: (i, b_col_blk)),
                  pl.BlockSpec((CONV_HALO, HALF), lambda i: (jnp.maximum(i * hb - 1, 0), a_col_blk)),
                  pl.BlockSpec((CONV_HALO, HALF), lambda i: (jnp.maximum(i * hb - 1, 0), b_col_blk)),
                  pl.BlockSpec((32, HALF), lambda i: (0, 0)),
                  pl.BlockSpec((1, HALF), lambda i: (0, 0)),
                  pl.BlockSpec((1, HALF), lambda i: (0, 0)),
                  pl.BlockSpec((1, HALF), lambda i: (0, 0))],
        out_specs=pl.BlockSpec((tm, HALF), lambda i: (i, 0)),
        out_shape=jax.ShapeDtypeStruct((s, HALF), BF16),
        scratch_shapes=[pltpu.VMEM((CONV_HALO + tm, HALF), F32),
                        pltpu.VMEM((SUBLANES - 1, CONV_HALO + tm, HALF), F32),
                        pltpu.VMEM((tm, HALF), F32)],
        compiler_params=_cparams(1),
        name="conv_module",
    )(z, z, z, z, cwp, conv_b, ln_g, ln_b)


def _alibi_operands(slopes):
    s2 = slopes.astype(F32) * LOG2E
    pieces = []
    rest = s2
    for _ in range(3):
        part = rest.astype(BF16).astype(F32)
        pieces.append(part)
        rest = rest - part
    pieces = jnp.broadcast_to(jnp.stack(pieces, axis=-1)[:, None, :], (N_HEADS, BLK, 3))
    pos = jnp.broadcast_to(jnp.arange(BLK, dtype=F32)[None, :, None], (N_HEADS, BLK, 3))
    pad = jnp.zeros((N_HEADS, BLK, HEAD_DIM - 6), F32)
    e_k = jnp.concatenate([pos, pieces, pad], axis=-1).astype(BF16)
    e_q = jnp.concatenate([pieces, -pos, pad], axis=-1).astype(BF16)
    return e_k, e_q


def _causal_tile():
    jj = lax.broadcasted_iota(I32, (BLK, BLK), 0)
    ii = lax.broadcasted_iota(I32, (BLK, BLK), 1)
    return jj <= ii


def _pipe_step(ops, m, acc, x=None, y=None, z=None):
    meta, score_tile, value_tile, valid = ops
    if x is not None:
        ix, s_x = x
        state_x, c2_x, flag_x, _ = meta(ix)
    if y is not None:
        iy, s_y, p_y, bm_y = y
        _, c2_y, flag_y, reset = meta(iy)
        m_old = m if reset is None else jnp.where(reset, NEG, m)
        m = jnp.maximum(m_old, bm_y)
        alpha_y = jnp.exp2(m_old - m)
    if z is not None:
        iz, p_z, alpha_z = z
        state_z, _, _, _ = meta(iz)
    bm_x = None
    pv = None
    for b in range(CHUNK):
        rows = slice(b * BLK, (b + 1) * BLK)
        if x is not None:
            t = score_tile(state_x, b)
            s_x[rows, :] = t
            bm = jnp.max(t, axis=0, keepdims=True) + c2_x[b]
            if flag_x[b] is not None:
                bm = jnp.where(flag_x[b], bm, NEG)
            bm_x = bm if bm_x is None else jnp.maximum(bm_x, bm)
        if y is not None:
            sub = m - c2_y[b]
            if flag_y[b] is not None:
                sub = jnp.where(flag_y[b], sub, -NEG)
            p_y[rows, :] = jnp.exp2(s_y[rows, :] - sub).astype(BF16)
        if z is not None:
            d = jnp.dot(value_tile(state_z, b), p_z[rows, :], preferred_element_type=F32)
            pv = d if pv is None else pv + d
    out = {}
    if x is not None:
        out["bm"] = jnp.where(valid(ix), bm_x, NEG)
    if y is not None:
        out["alpha"] = alpha_y
    if z is not None:
        acc = alpha_z * acc + pv
    return m, acc, out


def _load_state(state_ref):
    return tuple(state_ref[k, 0:1, :] for k in range(4))


def _store_state(state_ref, m, bm1, bm2, alpha):
    for k, v in enumerate((m, bm1, bm2, alpha)):
        state_ref[k] = jnp.broadcast_to(v, (SUBLANES, BLK))


def _pipe_trip(ops, slots, first_item, acc, emit=None):
    s_slots, p_slots, state_ref = slots[:4], slots[4:6], slots[6]
    m, bm1, bm2, alpha = _load_state(state_ref)
    for u in range(4):
        i = first_item + u
        m, acc, o = _pipe_step(ops, m, acc,
                               x=(i + 3, s_slots[(u + 3) % 4]),
                               y=(i + 1, s_slots[(u + 1) % 4], p_slots[(u + 1) % 2], bm1),
                               z=(i, p_slots[u % 2], alpha))
        if emit is not None:
            emit(i, acc)
        bm1, bm2, alpha = bm2, o["bm"], o["alpha"]
    _store_state(state_ref, m, bm1, bm2, alpha)
    return acc


def _pipe_fill(ops, slots):
    s_slots, p_slots, state_ref = slots[:4], slots[4:6], slots[6]
    m = jnp.full((1, BLK), NEG, F32)
    acc = jnp.zeros((V_ROWS, BLK), F32)
    m, acc, o0 = _pipe_step(ops, m, acc, x=(0, s_slots[0]))
    m, acc, o1 = _pipe_step(ops, m, acc, x=(1, s_slots[1]))
    m, acc, o2 = _pipe_step(ops, m, acc, x=(2, s_slots[2]), y=(0, s_slots[0], p_slots[0], o0["bm"]))
    _store_state(state_ref, m, o1["bm"], o2["bm"], o2["alpha"])
    return acc


def _attend(n, ops, slots):
    s_slots, p_slots, state_ref = slots[:4], slots[4:6], slots[6]
    acc = lax.fori_loop(0, n // 4, lambda j, a: _pipe_trip(ops, slots, 4 * j, a), _pipe_fill(ops, slots))
    base = (n // 4) * 4
    for u in range(3):
        def tail(acc=acc, u=u):
            m, bm1, bm2, alpha = _load_state(state_ref)
            m, acc_new, o = _pipe_step(ops, m, acc,
                                       y=(base + u + 1, s_slots[(u + 1) % 4], p_slots[(u + 1) % 2], bm1),
                                       z=(base + u, p_slots[u % 2], alpha))
            _store_state(state_ref, m, bm2, jnp.full((1, BLK), NEG, F32), o["alpha"])
            return acc_new
        acc = lax.cond(n % 4 > u, tail, lambda acc=acc: acc)
    return acc[0:HEAD_DIM] / acc[HEAD_DIM:HEAD_DIM + 1]


def _pipeline_slots():
    return ([pltpu.VMEM((CHUNK * BLK, BLK), F32)] * 4 + [pltpu.VMEM((CHUNK * BLK, BLK), BF16)] * 2
            + [pltpu.VMEM((4, SUBLANES, BLK), F32)])


def _scores_t(k_blk, q):
    return lax.dot_general(k_blk, q, (((1,), (1,)), ((), ())), preferred_element_type=F32)


def _moba_kernel(slopes_ref, q_ref, k_ref, vt_ref, ek_ref, eq_ref, o_ref, km_ref, sel_ref,
                 *slot_refs, nb, n_sel):
    h = pl.program_id(0)
    qi = pl.program_id(1)
    nbp = sel_ref.shape[0]

    @pl.when(qi == 0)
    def _():
        km_ref[...] = jnp.zeros_like(km_ref)

        def mean_body(b, carry):
            blk = k_ref[pl.ds(pl.multiple_of(b * BLK, BLK), BLK), :].astype(F32)
            km_ref[pl.ds(b, 1), :] = jnp.sum(blk, axis=0, keepdims=True) * (1.0 / BLK)
            return carry

        lax.fori_loop(0, nb, mean_body, 0)

    slope2 = slopes_ref[h] * LOG2E
    causal = _causal_tile()
    q = q_ref[...]
    q_ext = jnp.concatenate([q, eq_ref[0]], axis=1)
    e_k = ek_ref[0]

    g_t = lax.dot_general(km_ref[...], q.astype(F32), (((1,), (1,)), ((), ())),
                          precision=lax.Precision.HIGHEST, preferred_element_type=F32)
    row = lax.broadcasted_iota(I32, (nbp, BLK), 0)
    past = row < qi
    cur = jnp.where(past, g_t, NEG)
    picked = jnp.zeros((nbp, BLK), jnp.bool_)
    for _ in range(n_sel):
        mx = jnp.max(cur, axis=0, keepdims=True)
        first = jnp.min(jnp.where(cur == mx, row, nbp), axis=0, keepdims=True)
        hit = row == first
        picked = picked | hit
        cur = jnp.where(hit, -3e38, cur)
    sel_ref[...] = jnp.where(picked & past, 1.0, 0.0)

    n_main = qi // CHUNK

    def meta(c):
        c = jnp.minimum(c, n_main)
        tail = c == n_main
        kbs, c2s, flags = [], [], []
        for b in range(CHUNK):
            raw = jnp.where(tail, qi - (CHUNK - 1) + b, c * CHUNK + b)
            kb = jnp.clip(raw, 0, nb - 1)
            enabled = jnp.logical_or(jnp.logical_not(tail), raw >= n_main * CHUNK)
            sel_row = jnp.where(enabled, sel_ref[pl.ds(kb, 1), :], 0.0)
            if b == CHUNK - 1:
                sel_row = jnp.where(tail, 1.0, sel_row)
            kbs.append(kb)
            c2s.append((qi - kb).astype(F32) * (-slope2 * BLK))
            flags.append(sel_row > 0.5)
        return (tail, kbs), c2s, flags, None

    def score_tile(state, b):
        tail, kbs = state
        k_blk = k_ref[pl.ds(pl.multiple_of(kbs[b] * BLK, BLK), BLK), :]
        t = _scores_t(jnp.concatenate([k_blk, e_k], axis=1), q_ext)
        if b == CHUNK - 1:
            t = jnp.where(tail, jnp.where(causal, t, NEG), t)
        return t

    def value_tile(state, b):
        return vt_ref[0, state[1][b]]

    out_t = _attend(n_main + 1, (meta, score_tile, value_tile, lambda c: c <= n_main), slot_refs)
    o_ref[...] = jnp.transpose(out_t).astype(BF16)


def _moba(z, q_col_blk0, k_col_blk0, v_t, slopes, e_k, e_q):
    s = z.shape[0]
    nb = s // BLK
    assert nb % CHUNK == 0
    nbp = -(-nb // 8) * 8
    n_sel = max(1, min(MOBA_TOPK, nb - 1))
    grid_spec = pltpu.PrefetchScalarGridSpec(
        num_scalar_prefetch=1,
        grid=(N_HEADS, nb),
        in_specs=[pl.BlockSpec((BLK, HEAD_DIM), lambda h, i, sl: (i, q_col_blk0 + h)),
                  pl.BlockSpec((s, HEAD_DIM), lambda h, i, sl: (0, k_col_blk0 + h)),
                  pl.BlockSpec((1, nb, V_ROWS, BLK), lambda h, i, sl: (h, 0, 0, 0)),
                  pl.BlockSpec((1, BLK, HEAD_DIM), lambda h, i, sl: (h, 0, 0)),
                  pl.BlockSpec((1, BLK, HEAD_DIM), lambda h, i, sl: (h, 0, 0))],
        out_specs=pl.BlockSpec((BLK, HEAD_DIM), lambda h, i, sl: (i, h)),
        scratch_shapes=[pltpu.VMEM((nbp, HEAD_DIM), F32),
                        pltpu.VMEM((nbp, BLK), F32)] + _pipeline_slots(),
    )
    return pl.pallas_call(
        functools.partial(_moba_kernel, nb=nb, n_sel=n_sel),
        grid_spec=grid_spec,
        out_shape=jax.ShapeDtypeStruct((s, HALF), BF16),
        compiler_params=_cparams(2),
        name="moba_attention",
    )(slopes, z, z, v_t, e_k, e_q)


def _sortable(x):
    b = pltpu.bitcast(x, I32)
    return jnp.where(b < 0, b ^ 0x7FFFFFFF, b)


def _dsa_kernel(slopes_ref, qi_ref, wi_ref, ki_ref, q_ref, k_ref, vt_ref, ek_ref, eq_ref, o_ref,
                key_ref, half_ref, qm_ref, *slot_refs, n_sel):
    index_bits = max(1, (key_ref.shape[0] - 1).bit_length())
    qt = pl.program_id(0)
    h = pl.program_id(1)
    causal = _causal_tile()
    n_chunks = (qt + CHUNK) // CHUNK

    @pl.when(h == 0)
    def _():
        w_t = jnp.transpose(wi_ref[...].astype(F32))[0:IDX_HEADS, :]
        w_t = w_t * (IDX_HEADS ** -0.5 * IDX_DIM ** -0.5)
        lane = lax.broadcasted_iota(I32, (BLK, 128), 1)
        for hp in range(IDX_HEADS // 2):
            pair = qi_ref[:, hp * 128:(hp + 1) * 128]
            qm_ref[2 * hp] = jnp.where(lane < IDX_DIM, pair, jnp.zeros_like(pair))
            qm_ref[2 * hp + 1] = jnp.where(lane >= IDX_DIM, pair, jnp.zeros_like(pair))

        def score_body(c, carry):
            rows_c = pl.ds(pl.multiple_of(c * (CHUNK * BLK), CHUNK * BLK), CHUNK * BLK)
            ki_blk = ki_ref[rows_c, :]
            sc = jnp.zeros((CHUNK * BLK, BLK), F32)
            for hh in range(IDX_HEADS):
                lg = _scores_t(ki_blk, qm_ref[hh])
                sc = sc + w_t[hh:hh + 1, :] * jnp.maximum(lg, 0.0)
            sk = _sortable(sc)
            for b in range(CHUNK):
                kb = c * CHUNK + b
                blk = sk[b * BLK:(b + 1) * BLK]
                on_diag = jnp.where(causal, blk, INT_MIN)
                keys = jnp.where(kb < qt, blk, jnp.where(kb == qt, on_diag, INT_MIN))
                rows = pl.ds(pl.multiple_of(kb * BLK, BLK), BLK)
                key_ref[rows, :] = keys
                half_ref[rows, :] = lax.shift_right_arithmetic(keys, 16).astype(I16)
            return carry

        lax.fori_loop(0, n_chunks, score_body, 0)

        def count_ge16(cand):
            def cbody(c, cnts):
                chunk = half_ref.at[pl.ds(pl.multiple_of(c * (CHUNK * BLK), CHUNK * BLK), CHUNK * BLK), :]
                cnts = list(cnts)
                for r in range(CHUNK * BLK // 16):
                    hit = jnp.where(chunk[r * 16:(r + 1) * 16, :] >= cand, jnp.int16(1), jnp.int16(0))
                    cnts[r % len(cnts)] = cnts[r % len(cnts)] + hit
                return tuple(cnts)
            cnts = lax.fori_loop(0, n_chunks, cbody, tuple(jnp.zeros((16, BLK), I16) for _ in range(4)))
            tot = cnts[0].astype(I32) + cnts[1].astype(I32) + cnts[2].astype(I32) + cnts[3].astype(I32)
            return jnp.sum(tot, axis=0, keepdims=True)

        def bisect16(need):
            def bit_body(b, tb):
                cand_b = tb | lax.shift_left(jnp.int32(1), 15 - b)
                cnt = count_ge16((cand_b + I16_MIN).astype(I16))
                return jnp.where(cnt >= need, cand_b, tb)
            return lax.fori_loop(0, 16, bit_body, jnp.zeros((1, BLK), I32))

        t_hi = bisect16(n_sel) + I16_MIN
        above = count_ge16(jnp.minimum(t_hi + 1, -I16_MIN - 1).astype(I16))
        above = jnp.where(t_hi == -I16_MIN - 1, 0, above)

        def low_body(c, carry):
            rows = pl.ds(pl.multiple_of(c * (CHUNK * BLK), CHUNK * BLK), CHUNK * BLK)
            keys = key_ref[rows, :]
            low = (keys & 0xFFFF) + I16_MIN
            same = lax.shift_right_arithmetic(keys, 16) == t_hi
            half_ref[rows, :] = jnp.where(same, low, I16_MIN).astype(I16)
            return carry

        lax.fori_loop(0, n_chunks, low_body, 0)
        t_lo = bisect16(n_sel - above)
        thr = jnp.maximum(lax.shift_left(t_hi, 16) + t_lo, INT_MIN + 1)

        def count32(pred):
            def cbody(c, cnt):
                base = c * (CHUNK * BLK)
                keys = key_ref[pl.ds(pl.multiple_of(base, CHUNK * BLK), CHUNK * BLK), :]
                idx = base + lax.broadcasted_iota(I32, (CHUNK * BLK, BLK), 0)
                hit = jnp.where(pred(keys, idx), 1, 0).reshape(CHUNK * BLK // SUBLANES, SUBLANES, BLK)
                return cnt + jnp.sum(hit, axis=0)
            cnt = lax.fori_loop(0, n_chunks, cbody, jnp.zeros((SUBLANES, BLK), I32))
            return jnp.sum(cnt, axis=0, keepdims=True)

        excess = count32(lambda keys, idx: keys >= thr) - n_sel
        n_keys = n_chunks * (CHUNK * BLK)

        def tie_limit():
            def idx_body(c, carry):
                base = c * (CHUNK * BLK)
                rows = pl.ds(pl.multiple_of(base, CHUNK * BLK), CHUNK * BLK)
                idx = base + lax.broadcasted_iota(I32, (CHUNK * BLK, BLK), 0)
                half_ref[rows, :] = jnp.where(key_ref[rows, :] == thr, idx, I16_MAX).astype(I16)
                return carry

            lax.fori_loop(0, n_chunks, idx_body, 0)
            tied = n_keys - count_ge16(jnp.full((1, BLK), I16_MAX, I16))
            keep = tied - excess
            def bit_body(b, j):
                cand = j | lax.shift_left(jnp.int32(1), index_bits - 1 - b)
                below = n_keys - count_ge16(cand.astype(I16))
                return jnp.where(below <= keep - 1, cand, j)
            j = lax.fori_loop(0, index_bits, bit_body, jnp.zeros((1, BLK), I32))
            return jnp.where(excess > 0, j, n_keys)

        def mask_ties():
            last_tied = tie_limit()

            def body(c, carry):
                base = c * (CHUNK * BLK)
                rows = pl.ds(pl.multiple_of(base, CHUNK * BLK), CHUNK * BLK)
                keys = key_ref[rows, :]
                idx = base + lax.broadcasted_iota(I32, (CHUNK * BLK, BLK), 0)
                chosen = (keys > thr) | ((keys == thr) & (idx <= last_tied))
                key_ref[rows, :] = jnp.where(chosen, 0, NEG_BITS)
                return carry

            lax.fori_loop(0, n_chunks, body, 0)

        def mask_plain():
            def body(c, carry):
                rows = pl.ds(pl.multiple_of(c * (CHUNK * BLK), CHUNK * BLK), CHUNK * BLK)
                key_ref[rows, :] = jnp.where(key_ref[rows, :] >= thr, 0, NEG_BITS)
                return carry

            lax.fori_loop(0, n_chunks, body, 0)

        lax.cond(jnp.max(excess) > 0, mask_ties, mask_plain)

    slope2 = slopes_ref[h] * LOG2E
    q_ext = jnp.concatenate([q_ref[...], eq_ref[0]], axis=1)
    e_k = ek_ref[0]

    no_flags = [None] * CHUNK

    def chunk_meta(c):
        kbs = [jnp.minimum(c, n_chunks - 1) * CHUNK + b for b in range(CHUNK)]
        return kbs, [jnp.maximum(qt - kb, 0).astype(F32) * (-slope2 * BLK) for kb in kbs], no_flags, None

    def score_tile(kbs, b):
        rows = pl.ds(pl.multiple_of(kbs[b] * BLK, BLK), BLK)
        k_ext = jnp.concatenate([k_ref[rows, :], e_k], axis=1)
        return _scores_t(k_ext, q_ext) + pltpu.bitcast(key_ref[rows, :], F32)

    def value_tile(kbs, b):
        return vt_ref[0, kbs[b]]

    out_t = _attend(n_chunks, (chunk_meta, score_tile, value_tile, lambda c: c < n_chunks), slot_refs)
    o_ref[...] = jnp.transpose(out_t).astype(BF16)


def _dsa(z, v_t, slopes, e_k, e_q):
    s = z.shape[0]
    nb = s // BLK
    assert nb % CHUNK == 0 and s <= I16_MAX
    n_sel = min(DSA_TOPK_MAX, s // 4)
    grid_spec = pltpu.PrefetchScalarGridSpec(
        num_scalar_prefetch=1,
        grid=(nb, N_HEADS),
        in_specs=[pl.BlockSpec((BLK, IDX_HEADS * IDX_DIM), lambda i, h, sl: (i, OD_QI // (IDX_HEADS * IDX_DIM))),
                  pl.BlockSpec((BLK, 128), lambda i, h, sl: (i, OD_WI // 128)),
                  pl.BlockSpec((s, 128), lambda i, h, sl: (0, OD_KI // 128)),
                  pl.BlockSpec((BLK, HEAD_DIM), lambda i, h, sl: (i, h)),
                  pl.BlockSpec((s, HEAD_DIM), lambda i, h, sl: (0, N_HEADS + h)),
                  pl.BlockSpec((1, nb, V_ROWS, BLK), lambda i, h, sl: (h, 0, 0, 0)),
                  pl.BlockSpec((1, BLK, HEAD_DIM), lambda i, h, sl: (h, 0, 0)),
                  pl.BlockSpec((1, BLK, HEAD_DIM), lambda i, h, sl: (h, 0, 0))],
        out_specs=pl.BlockSpec((BLK, HEAD_DIM), lambda i, h, sl: (i, h)),
        scratch_shapes=[pltpu.VMEM((s, BLK), I32),
                        pltpu.VMEM((s, BLK), I16),
                        pltpu.VMEM((IDX_HEADS, BLK, 128), BF16)] + _pipeline_slots(),
    )
    return pl.pallas_call(
        functools.partial(_dsa_kernel, n_sel=n_sel),
        grid_spec=grid_spec,
        out_shape=jax.ShapeDtypeStruct((s, HALF), BF16),
        compiler_params=_cparams(2),
        name="dsa_attention",
    )(slopes, z, z, z, z, z, v_t, e_k, e_q)


def _v_transposed(z, col0):
    s = z.shape[0]
    nb = s // BLK
    v = z[:, col0:col0 + HALF].reshape(nb, BLK, N_HEADS, HEAD_DIM)
    v_t = jnp.transpose(v, (2, 0, 3, 1))
    ones = jnp.ones((N_HEADS, nb, 8, BLK), v_t.dtype)
    return jnp.concatenate([v_t, ones, jnp.zeros((N_HEADS, nb, V_ROWS - HEAD_DIM - 8, BLK), v_t.dtype)], axis=2)


def _qk_gain(qn, kn, col0, n):
    gq = jnp.tile(qn.astype(F32), N_HEADS) * (HEAD_DIM ** -0.5 * LOG2E)
    gk = jnp.tile(kn.astype(F32), N_HEADS)
    return jnp.concatenate([jnp.ones((col0,), F32), gq, gk, jnp.ones((n - col0 - 2 * HALF,), F32)])[None, :]


def kernel(x, c, norm_g, ada_w, ada_b, w_out, ev_w_in, pool_w, pool_scale, moba_qn, moba_kn,
           od_w_in, dsa_qn, dsa_kn, conv_w, conv_b, conv_ln_g, conv_ln_b):
    b, s, d = x.shape
    assert b == 1 and d == D_MODEL and s % BLK == 0
    depth = norm_g.shape[0]
    x2 = x[0].astype(F32)
    mods = _mods(c, ada_w, ada_b)[:, 0:1, :]
    slopes = jnp.exp2(-8.0 * jnp.arange(1, N_HEADS + 1, dtype=F32) / N_HEADS)
    e_k, e_q = _alibi_operands(slopes)

    for l in range(depth):
        shift, scl, gate_res = (mods[l, :, i * d:(i + 1) * d] for i in range(3))
        g = norm_g[l][None, :].astype(F32)
        if l % 2 == 0:
            e = l // 2
            w = ev_w_in[e].astype(BF16)
            z = _proj(x2, g, scl, shift, w, _qk_gain(moba_qn[e], moba_kn[e], HALF, EV_N), qk_col0=HALF)
            ya = _pool(z, 0, pool_w[e].astype(BF16), pool_scale[e][None, :].astype(F32))
            yb = _moba(z, N_HEADS, 2 * N_HEADS, _v_transposed(z, 3 * HALF), slopes, e_k, e_q)
            g_blk = 4
        else:
            o = l // 2
            w = od_w_in[o]
            c_qi = 3 * HALF
            c_wi = c_qi + IDX_HEADS * IDX_DIM
            c_ki = c_wi + IDX_HEADS
            c_ga = c_ki + IDX_DIM
            ki_w = w[:, c_ki:c_ga]
            w = jnp.concatenate([
                w[:, 0:3 * HALF], w[:, c_ga:],
                w[:, c_qi:c_wi], ki_w, ki_w,
                w[:, c_wi:c_ki], jnp.zeros((d, OD_N - OD_WI - IDX_HEADS), w.dtype)], axis=1).astype(BF16)
            z = _proj(x2, g, scl, shift, w, _qk_gain(dsa_qn[o], dsa_kn[o], 0, OD_N), qk_col0=0)
            ya = _dsa(z, _v_transposed(z, 2 * HALF), slopes, e_k, e_q)
            yb = _conv(z, 3, 4, conv_w[o].astype(F32), conv_b[o][None, :].astype(F32),
                       conv_ln_g[o][None, :].astype(F32), conv_ln_b[o][None, :].astype(F32))
            g_blk = 5
        x2 = _outproj(ya, yb, z, g_blk, x2, gate_res, w_out[l].astype(BF16))
    return x2[None].astype(x.dtype)
```

```python
import functools

import jax
import numpy as np
import jax.numpy as jnp
from jax import lax
from jax.experimental import pallas as pl
from jax.experimental.pallas import tpu as pltpu

F32 = jnp.float32
BF16 = jnp.bfloat16
I32 = jnp.int32
I16 = jnp.int16

D_MODEL = 2048
HALF = D_MODEL // 2
HEAD_DIM = 128
N_HEADS = HALF // HEAD_DIM
POOL_WINDOWS = (2, 4, 8, 16)
POOL_GW = HALF // len(POOL_WINDOWS)
POOL_HALO = 16
BLK = 256
CHUNK = 4
MOBA_TOPK = 3
DSA_TOPK_MAX = 256
IDX_HEADS = 8
IDX_DIM = 64
CONV_WIDTH = 31
CONV_HALO = 32
SUBLANES = 8
EPS = 1e-6
NEG = -1e30
LOG2E = 1.4426950408889634
INT_MIN = -(2 ** 31)
I16_MIN = -(2 ** 15)
I16_MAX = 2 ** 15 - 1
NEG_BITS = int(np.array(NEG, np.float32).view(np.int32))
V_ROWS = HEAD_DIM + 16

EV_N = 4 * HALF + D_MODEL
OD_N = 8192
OD_QI = 7 * HALF
OD_KI = OD_QI + IDX_HEADS * IDX_DIM
OD_WI = OD_KI + 128

VMEM_LIMIT = 56 * 1024 * 1024


def _cparams(n_axes, vmem=VMEM_LIMIT):
    return pltpu.CompilerParams(dimension_semantics=("arbitrary",) * n_axes, vmem_limit_bytes=vmem)


def _sigmoid(x):
    return 1.0 / (1.0 + jnp.exp(-x))


def _mods_kernel(c_ref, w_ref, b_ref, o_ref):
    c = c_ref[...]
    ca = c * _sigmoid(c)
    o_ref[0] = jnp.dot(ca, w_ref[0], precision=lax.Precision.HIGHEST,
                       preferred_element_type=F32) + b_ref[0]


def _mods(c, ada_w, ada_b):
    depth, d, n = ada_w.shape
    tn = 768
    c8 = jnp.broadcast_to(c.astype(F32), (8, d))
    return pl.pallas_call(
        _mods_kernel,
        grid=(depth, n // tn),
        in_specs=[pl.BlockSpec((8, d), lambda l, j: (0, 0)),
                  pl.BlockSpec((1, d, tn), lambda l, j: (l, 0, j)),
                  pl.BlockSpec((1, 1, tn), lambda l, j: (l, 0, j))],
        out_specs=pl.BlockSpec((1, 8, tn), lambda l, j: (l, 0, j)),
        out_shape=jax.ShapeDtypeStruct((depth, 8, n), F32),
        compiler_params=_cparams(2),
        name="adaln_mods",
    )(c8, ada_w, ada_b.reshape(depth, 1, n))


def _proj_kernel(x_ref, g_ref, scl_ref, sh_ref, w_ref, gain_ref, o_ref, h_ref, *, norm_tiles, tn):
    j = pl.program_id(1)

    @pl.when(j == 0)
    def _():
        x = x_ref[...]
        y = x * lax.rsqrt(jnp.mean(x * x, axis=-1, keepdims=True) + EPS)
        h = (y * g_ref[...]) * (1.0 + scl_ref[...]) + sh_ref[...]
        h_ref[...] = h.astype(BF16)

    z = jnp.dot(h_ref[...], w_ref[...], preferred_element_type=F32)

    is_qk = jnp.logical_and(j >= norm_tiles[0], j < norm_tiles[1])

    @pl.when(is_qk)
    def _():
        for c in range(tn // HEAD_DIM):
            cs = slice(c * HEAD_DIM, (c + 1) * HEAD_DIM)
            zc = z[:, cs]
            r = lax.rsqrt(jnp.mean(zc * zc, axis=-1, keepdims=True) + EPS)
            o_ref[:, cs] = (zc * r * gain_ref[:, cs]).astype(BF16)

    @pl.when(jnp.logical_not(is_qk))
    def _():
        o_ref[...] = z.astype(BF16)


def _proj(x2, g, scl, sh, w, gain, *, qk_col0):
    s, d = x2.shape
    n = w.shape[1]
    tm = min(1024, s)
    tn = 512
    return pl.pallas_call(
        functools.partial(_proj_kernel, norm_tiles=(qk_col0 // tn, (qk_col0 + 2 * HALF) // tn), tn=tn),
        grid=(s // tm, n // tn),
        in_specs=[pl.BlockSpec((tm, d), lambda i, j: (i, 0)),
                  pl.BlockSpec((1, d), lambda i, j: (0, 0)),
                  pl.BlockSpec((1, d), lambda i, j: (0, 0)),
                  pl.BlockSpec((1, d), lambda i, j: (0, 0)),
                  pl.BlockSpec((d, tn), lambda i, j: (0, j)),
                  pl.BlockSpec((1, tn), lambda i, j: (0, j))],
        out_specs=pl.BlockSpec((tm, tn), lambda i, j: (i, j)),
        out_shape=jax.ShapeDtypeStruct((s, n), BF16),
        scratch_shapes=[pltpu.VMEM((tm, d), BF16)],
        compiler_params=_cparams(2),
        name="norm_proj",
    )(x2, g, scl, sh, w, gain)


def _out_kernel(ya_ref, yb_ref, ga_ref, gb_ref, x_ref, gr_ref, w_ref, o_ref, a_ref):
    j = pl.program_id(1)

    @pl.when(j == 0)
    def _():
        for y_ref, g_ref, cs in ((ya_ref, ga_ref, slice(0, HALF)), (yb_ref, gb_ref, slice(HALF, 2 * HALF))):
            g = g_ref[...].astype(F32)
            a_ref[:, cs] = (y_ref[...].astype(F32) * (g * _sigmoid(g))).astype(BF16)

    acc = jnp.dot(a_ref[...], w_ref[...], preferred_element_type=F32)
    o_ref[...] = x_ref[...] + gr_ref[...] * acc


def _outproj(ya, yb, z, g_col_blk, x2, gate_res, w):
    s, d = x2.shape
    tm = min(1024, s)
    tn = 512
    return pl.pallas_call(
        _out_kernel,
        grid=(s // tm, d // tn),
        in_specs=[pl.BlockSpec((tm, HALF), lambda i, j: (i, 0)),
                  pl.BlockSpec((tm, HALF), lambda i, j: (i, 0)),
                  pl.BlockSpec((tm, HALF), lambda i, j: (i, g_col_blk)),
                  pl.BlockSpec((tm, HALF), lambda i, j: (i, g_col_blk + 1)),
                  pl.BlockSpec((tm, tn), lambda i, j: (i, j)),
                  pl.BlockSpec((1, tn), lambda i, j: (0, j)),
                  pl.BlockSpec((2 * HALF, tn), lambda i, j: (0, j))],
        out_specs=pl.BlockSpec((tm, tn), lambda i, j: (i, j)),
        out_shape=jax.ShapeDtypeStruct((s, d), F32),
        scratch_shapes=[pltpu.VMEM((tm, 2 * HALF), BF16)],
        compiler_params=_cparams(2),
        name="gated_outproj",
    )(ya, yb, z, z, x2, gate_res, w)


def _pool_kernel(u_ref, halo_ref, w_ref, ps_ref, o_ref, ext_ref, *, tm):
    i = pl.program_id(0)
    halo = halo_ref[...].astype(F32)
    ext_ref[0:POOL_HALO, :] = jnp.where(i > 0, halo, 0.0)
    ext_ref[POOL_HALO:POOL_HALO + tm, :] = u_ref[...].astype(F32)
    t1 = (i * tm + 1 + lax.broadcasted_iota(I32, (tm, 1), 0)).astype(F32)
    for g, win in enumerate(POOL_WINDOWS):
        cs = slice(g * POOL_GW, (g + 1) * POOL_GW)
        u = ext_ref[POOL_HALO:POOL_HALO + tm, cs]
        acc = u
        for j in range(1, win):
            acc = acc + ext_ref[POOL_HALO - j:POOL_HALO - j + tm, cs]
        dlt = acc / jnp.minimum(t1, float(win)) - u
        y = jnp.dot(dlt.astype(BF16), w_ref[g], preferred_element_type=F32)
        o_ref[:, cs] = (y * ps_ref[:, cs]).astype(BF16)


def _pool(z, u_col_blk, pool_w, pool_scale):
    s = z.shape[0]
    tm = min(512, s)
    hb = tm // POOL_HALO
    return pl.pallas_call(
        functools.partial(_pool_kernel, tm=tm),
        grid=(s // tm,),
        in_specs=[pl.BlockSpec((tm, HALF), lambda i: (i, u_col_blk)),
                  pl.BlockSpec((POOL_HALO, HALF),
                               lambda i: (jnp.maximum(i * hb - 1, 0), u_col_blk)),
                  pl.BlockSpec((len(POOL_WINDOWS), POOL_GW, POOL_GW), lambda i: (0, 0, 0)),
                  pl.BlockSpec((1, HALF), lambda i: (0, 0))],
        out_specs=pl.BlockSpec((tm, HALF), lambda i: (i, 0)),
        out_shape=jax.ShapeDtypeStruct((s, HALF), BF16),
        scratch_shapes=[pltpu.VMEM((POOL_HALO + tm, HALF), F32)],
        compiler_params=_cparams(1),
        name="pool_mixer",
    )(z, z, pool_w, pool_scale)


def _conv_kernel(a_ref, b_ref, ah_ref, bh_ref, cw_ref, cb_ref, lg_ref, lb_ref, o_ref,
                 ext_ref, sh_ref, y_ref, *, tm):
    i = pl.program_id(0)

    def glu(a, b):
        return a.astype(F32) * _sigmoid(b.astype(F32))

    ext_ref[0:CONV_HALO, :] = jnp.where(i > 0, glu(ah_ref[...], bh_ref[...]), 0.0)
    ext_ref[CONV_HALO:CONV_HALO + tm, :] = glu(a_ref[...], b_ref[...])
    span = CONV_HALO + tm - SUBLANES
    for r in range(1, SUBLANES):
        sh_ref[r - 1, 0:span, :] = ext_ref[r:r + span, :]
    off = CONV_HALO - (CONV_WIDTH - 1)
    for ct in range(HALF // 128):
        cs = slice(ct * 128, (ct + 1) * 128)
        acc = None
        for j in range(CONV_WIDTH):
            r, base = (off + j) % SUBLANES, (off + j) // SUBLANES * SUBLANES
            src = ext_ref[base:base + tm, cs] if r == 0 else sh_ref[r - 1, base:base + tm, cs]
            term = cw_ref[j:j + 1, cs] * src
            acc = term if acc is None else acc + term
        y_ref[:, cs] = acc + cb_ref[:, cs]
    y = y_ref[...]
    mu = jnp.mean(y, axis=-1, keepdims=True)
    yc = y - mu
    var = jnp.mean(yc * yc, axis=-1, keepdims=True)
    yn = yc * lax.rsqrt(var + EPS) * lg_ref[...] + lb_ref[...]
    o_ref[...] = (yn * _sigmoid(yn)).astype(BF16)


def _conv(z, a_col_blk, b_col_blk, conv_w, conv_b, ln_g, ln_b):
    s = z.shape[0]
    tm = min(256, s)
    hb = tm // CONV_HALO
    cwp = jnp.zeros((32, HALF), F32).at[:CONV_WIDTH].set(conv_w)
    return pl.pallas_call(
        functools.partial(_conv_kernel, tm=tm),
        grid=(s // tm,),
        in_specs=[pl.BlockSpec((tm, HALF), lambda i: (i, a_col_blk)),
                  pl.BlockSpec((tm, HALF), lambda i: (i, b_col_blk)),
                  pl.BlockSpec((CONV_HALO, HALF), lambda i: (jnp.maximum(i * hb - 1, 0), a_col_blk)),
                  pl.BlockSpec((CONV_HALO, HALF), lambda i: (jnp.maximum(i * hb - 1, 0), b_col_blk)),
                  pl.BlockSpec((32, HALF), lambda i: (0, 0)),
                  pl.BlockSpec((1, HALF), lambda i: (0, 0)),
                  pl.BlockSpec((1, HALF), lambda i: (0, 0)),
                  pl.BlockSpec((1, HALF), lambda i: (0, 0))],
        out_specs=pl.BlockSpec((tm, HALF), lambda i: (i, 0)),
        out_shape=jax.ShapeDtypeStruct((s, HALF), BF16),
        scratch_shapes=[pltpu.VMEM((CONV_HALO + tm, HALF), F32),
                        pltpu.VMEM((SUBLANES - 1, CONV_HALO + tm, HALF), F32),
                        pltpu.VMEM((tm, HALF), F32)],
        compiler_params=_cparams(1),
        name="conv_module",
    )(z, z, z, z, cwp, conv_b, ln_g, ln_b)


def _alibi_operands(slopes):
    s2 = slopes.astype(F32) * LOG2E
    pieces = []
    rest = s2
    for _ in range(3):
        part = rest.astype(BF16).astype(F32)
        pieces.append(part)
        rest = rest - part
    pieces = jnp.broadcast_to(jnp.stack(pieces, axis=-1)[:, None, :], (N_HEADS, BLK, 3))
    pos = jnp.broadcast_to(jnp.arange(BLK, dtype=F32)[None, :, None], (N_HEADS, BLK, 3))
    pad = jnp.zeros((N_HEADS, BLK, HEAD_DIM - 6), F32)
    e_k = jnp.concatenate([pos, pieces, pad], axis=-1).astype(BF16)
    e_q = jnp.concatenate([pieces, -pos, pad], axis=-1).astype(BF16)
    return e_k, e_q


def _causal_tile():
    jj = lax.broadcasted_iota(I32, (BLK, BLK), 0)
    ii = lax.broadcasted_iota(I32, (BLK, BLK), 1)
    return jj <= ii


def _pipe_step(ops, m, acc, x=None, y=None, z=None):
    meta, score_tile, value_tile, valid = ops
    if x is not None:
        ix, s_x = x
        state_x, c2_x, flag_x, _ = meta(ix)
    if y is not None:
        iy, s_y, p_y, bm_y = y
        _, c2_y, flag_y, reset = meta(iy)
        m_old = m if reset is None else jnp.where(reset, NEG, m)
        m = jnp.maximum(m_old, bm_y)
        alpha_y = jnp.exp2(m_old - m)
    if z is not None:
        iz, p_z, alpha_z = z
        state_z, _, _, _ = meta(iz)
    bm_x = None
    pv = None
    for b in range(CHUNK):
        rows = slice(b * BLK, (b + 1) * BLK)
        if x is not None:
            t = score_tile(state_x, b)
            s_x[rows, :] = t
            bm = jnp.max(t, axis=0, keepdims=True) + c2_x[b]
            if flag_x[b] is not None:
                bm = jnp.where(flag_x[b], bm, NEG)
            bm_x = bm if bm_x is None else jnp.maximum(bm_x, bm)
        if y is not None:
            sub = m - c2_y[b]
            if flag_y[b] is not None:
                sub = jnp.where(flag_y[b], sub, -NEG)
            p_y[rows, :] = jnp.exp2(s_y[rows, :] - sub).astype(BF16)
        if z is not None:
            d = jnp.dot(value_tile(state_z, b), p_z[rows, :], preferred_element_type=F32)
            pv = d if pv is None else pv + d
    out = {}
    if x is not None:
        out["bm"] = jnp.where(valid(ix), bm_x, NEG)
    if y is not None:
        out["alpha"] = alpha_y
    if z is not None:
        acc = alpha_z * acc + pv
    return m, acc, out


def _load_state(state_ref):
    return tuple(state_ref[k, 0:1, :] for k in range(4))


def _store_state(state_ref, m, bm1, bm2, alpha):
    for k, v in enumerate((m, bm1, bm2, alpha)):
        state_ref[k] = jnp.broadcast_to(v, (SUBLANES, BLK))


def _pipe_trip(ops, slots, first_item, acc, emit=None):
    s_slots, p_slots, state_ref = slots[:4], slots[4:6], slots[6]
    m, bm1, bm2, alpha = _load_state(state_ref)
    for u in range(4):
        i = first_item + u
        m, acc, o = _pipe_step(ops, m, acc,
                               x=(i + 3, s_slots[(u + 3) % 4]),
                               y=(i + 1, s_slots[(u + 1) % 4], p_slots[(u + 1) % 2], bm1),
                               z=(i, p_slots[u % 2], alpha))
        if emit is not None:
            emit(i, acc)
        bm1, bm2, alpha = bm2, o["bm"], o["alpha"]
    _store_state(state_ref, m, bm1, bm2, alpha)
    return acc


def _pipe_fill(ops, slots):
    s_slots, p_slots, state_ref = slots[:4], slots[4:6], slots[6]
    m = jnp.full((1, BLK), NEG, F32)
    acc = jnp.zeros((V_ROWS, BLK), F32)
    m, acc, o0 = _pipe_step(ops, m, acc, x=(0, s_slots[0]))
    m, acc, o1 = _pipe_step(ops, m, acc, x=(1, s_slots[1]))
    m, acc, o2 = _pipe_step(ops, m, acc, x=(2, s_slots[2]), y=(0, s_slots[0], p_slots[0], o0["bm"]))
    _store_state(state_ref, m, o1["bm"], o2["bm"], o2["alpha"])
    return acc


def _attend(n, ops, slots):
    s_slots, p_slots, state_ref = slots[:4], slots[4:6], slots[6]
    acc = lax.fori_loop(0, n // 4, lambda j, a: _pipe_trip(ops, slots, 4 * j, a), _pipe_fill(ops, slots))
    base = (n // 4) * 4
    for u in range(3):
        def tail(acc=acc, u=u):
            m, bm1, bm2, alpha = _load_state(state_ref)
            if u < 2:
                m, acc_new, o = _pipe_step(ops, m, acc,
                                           y=(base + u + 1, s_slots[(u + 1) % 4], p_slots[(u + 1) % 2], bm1),
                                           z=(base + u, p_slots[u % 2], alpha))
                _store_state(state_ref, m, bm2, jnp.full((1, BLK), NEG, F32), o["alpha"])
            else:
                m, acc_new, _ = _pipe_step(ops, m, acc, z=(base + u, p_slots[u % 2], alpha))
            return acc_new
        acc = lax.cond(n % 4 > u, tail, lambda acc=acc: acc)
    return acc[0:HEAD_DIM] / acc[HEAD_DIM:HEAD_DIM + 1]


def _pipeline_slots():
    return ([pltpu.VMEM((CHUNK * BLK, BLK), F32)] * 4 + [pltpu.VMEM((CHUNK * BLK, BLK), BF16)] * 2
            + [pltpu.VMEM((4, SUBLANES, BLK), F32)])


def _scores_t(k_blk, q):
    return lax.dot_general(k_blk, q, (((1,), (1,)), ((), ())), preferred_element_type=F32)


def _moba_kernel(slopes_ref, q_ref, k_ref, vt_ref, ek_ref, eq_ref, o_ref, km_ref, sel_ref,
                 *slot_refs, nb, n_sel):
    h = pl.program_id(0)
    qi = pl.program_id(1)
    nbp = sel_ref.shape[0]

    @pl.when(qi == 0)
    def _():
        km_ref[...] = jnp.zeros_like(km_ref)

        def mean_body(b, carry):
            blk = k_ref[pl.ds(pl.multiple_of(b * BLK, BLK), BLK), :].astype(F32)
            km_ref[pl.ds(b, 1), :] = jnp.sum(blk, axis=0, keepdims=True) * (1.0 / BLK)
            return carry

        lax.fori_loop(0, nb, mean_body, 0)

    slope2 = slopes_ref[h] * LOG2E
    causal = _causal_tile()
    q = q_ref[...]
    q_ext = jnp.concatenate([q, eq_ref[0]], axis=1)
    e_k = ek_ref[0]

    g_t = lax.dot_general(km_ref[...], q.astype(F32), (((1,), (1,)), ((), ())),
                          precision=lax.Precision.HIGHEST, preferred_element_type=F32)
    row = lax.broadcasted_iota(I32, (nbp, BLK), 0)
    past = row < qi
    cur = jnp.where(past, g_t, NEG)
    picked = jnp.zeros((nbp, BLK), jnp.bool_)
    for _ in range(n_sel):
        mx = jnp.max(cur, axis=0, keepdims=True)
        first = jnp.min(jnp.where(cur == mx, row, nbp), axis=0, keepdims=True)
        hit = row == first
        picked = picked | hit
        cur = jnp.where(hit, -3e38, cur)
    sel_ref[...] = jnp.where(picked & past, 1.0, 0.0)

    n_main = qi // CHUNK

    def meta(c):
        c = jnp.minimum(c, n_main)
        tail = c == n_main
        kbs, c2s, flags = [], [], []
        for b in range(CHUNK):
            raw = jnp.where(tail, qi - (CHUNK - 1) + b, c * CHUNK + b)
            kb = jnp.clip(raw, 0, nb - 1)
            enabled = jnp.logical_or(jnp.logical_not(tail), raw >= n_main * CHUNK)
            sel_row = jnp.where(enabled, sel_ref[pl.ds(kb, 1), :], 0.0)
            if b == CHUNK - 1:
                sel_row = jnp.where(tail, 1.0, sel_row)
            kbs.append(kb)
            c2s.append((qi - kb).astype(F32) * (-slope2 * BLK))
            flags.append(sel_row > 0.5)
        return (tail, kbs), c2s, flags, None

    def score_tile(state, b):
        tail, kbs = state
        k_blk = k_ref[pl.ds(pl.multiple_of(kbs[b] * BLK, BLK), BLK), :]
        t = _scores_t(jnp.concatenate([k_blk, e_k], axis=1), q_ext)
        if b == CHUNK - 1:
            t = jnp.where(tail, jnp.where(causal, t, NEG), t)
        return t

    def value_tile(state, b):
        return vt_ref[0, state[1][b]]

    out_t = _attend(n_main + 1, (meta, score_tile, value_tile, lambda c: c <= n_main), slot_refs)
    o_ref[...] = jnp.transpose(out_t).astype(BF16)


def _moba(z, q_col_blk0, k_col_blk0, v_t, slopes, e_k, e_q):
    s = z.shape[0]
    nb = s // BLK
    assert nb % CHUNK == 0
    nbp = -(-nb // 8) * 8
    n_sel = max(1, min(MOBA_TOPK, nb - 1))
    grid_spec = pltpu.PrefetchScalarGridSpec(
        num_scalar_prefetch=1,
        grid=(N_HEADS, nb),
        in_specs=[pl.BlockSpec((BLK, HEAD_DIM), lambda h, i, sl: (i, q_col_blk0 + h)),
                  pl.BlockSpec((s, HEAD_DIM), lambda h, i, sl: (0, k_col_blk0 + h)),
                  pl.BlockSpec((1, nb, V_ROWS, BLK), lambda h, i, sl: (h, 0, 0, 0)),
                  pl.BlockSpec((1, BLK, HEAD_DIM), lambda h, i, sl: (h, 0, 0)),
                  pl.BlockSpec((1, BLK, HEAD_DIM), lambda h, i, sl: (h, 0, 0))],
        out_specs=pl.BlockSpec((BLK, HEAD_DIM), lambda h, i, sl: (i, h)),
        scratch_shapes=[pltpu.VMEM((nbp, HEAD_DIM), F32),
                        pltpu.VMEM((nbp, BLK), F32)] + _pipeline_slots(),
    )
    return pl.pallas_call(
        functools.partial(_moba_kernel, nb=nb, n_sel=n_sel),
        grid_spec=grid_spec,
        out_shape=jax.ShapeDtypeStruct((s, HALF), BF16),
        compiler_params=_cparams(2),
        name="moba_attention",
    )(slopes, z, z, v_t, e_k, e_q)


def _sortable(x):
    b = pltpu.bitcast(x, I32)
    return jnp.where(b < 0, b ^ 0x7FFFFFFF, b)


def _dsa_kernel(slopes_ref, qi_ref, wi_ref, ki_ref, q_ref, k_ref, vt_ref, ek_ref, eq_ref, o_ref,
                key_ref, half_ref, qm_ref, *slot_refs, n_sel):
    index_bits = max(1, (key_ref.shape[0] - 1).bit_length())
    qt = pl.program_id(0)
    h = pl.program_id(1)
    causal = _causal_tile()
    n_chunks = (qt + CHUNK) // CHUNK

    @pl.when(h == 0)
    def _():
        w_t = jnp.transpose(wi_ref[...].astype(F32))[0:IDX_HEADS, :]
        w_t = w_t * (IDX_HEADS ** -0.5 * IDX_DIM ** -0.5)
        lane = lax.broadcasted_iota(I32, (BLK, 128), 1)
        for hp in range(IDX_HEADS // 2):
            pair = qi_ref[:, hp * 128:(hp + 1) * 128]
            qm_ref[2 * hp] = jnp.where(lane < IDX_DIM, pair, jnp.zeros_like(pair))
            qm_ref[2 * hp + 1] = jnp.where(lane >= IDX_DIM, pair, jnp.zeros_like(pair))

        def score_body(c, carry):
            rows_c = pl.ds(pl.multiple_of(c * (CHUNK * BLK), CHUNK * BLK), CHUNK * BLK)
            ki_blk = ki_ref[rows_c, :]
            sc = jnp.zeros((CHUNK * BLK, BLK), F32)
            for hh in range(IDX_HEADS):
                lg = _scores_t(ki_blk, qm_ref[hh])
                sc = sc + w_t[hh:hh + 1, :] * jnp.maximum(lg, 0.0)
            sk = _sortable(sc)
            for b in range(CHUNK):
                kb = c * CHUNK + b
                blk = sk[b * BLK:(b + 1) * BLK]
                on_diag = jnp.where(causal, blk, INT_MIN)
                keys = jnp.where(kb < qt, blk, jnp.where(kb == qt, on_diag, INT_MIN))
                rows = pl.ds(pl.multiple_of(kb * BLK, BLK), BLK)
                key_ref[rows, :] = keys
                half_ref[rows, :] = lax.shift_right_arithmetic(keys, 16).astype(I16)
            return carry

        lax.fori_loop(0, n_chunks, score_body, 0)

        def count_ge16(cand):
            def cbody(c, cnts):
                chunk = half_ref.at[pl.ds(pl.multiple_of(c * (CHUNK * BLK), CHUNK * BLK), CHUNK * BLK), :]
                cnts = list(cnts)
                for r in range(CHUNK * BLK // 16):
                    hit = jnp.where(chunk[r * 16:(r + 1) * 16, :] >= cand, jnp.int16(1), jnp.int16(0))
                    cnts[r % len(cnts)] = cnts[r % len(cnts)] + hit
                return tuple(cnts)
            cnts = lax.fori_loop(0, n_chunks, cbody, tuple(jnp.zeros((16, BLK), I16) for _ in range(4)))
            tot = cnts[0].astype(I32) + cnts[1].astype(I32) + cnts[2].astype(I32) + cnts[3].astype(I32)
            return jnp.sum(tot, axis=0, keepdims=True)

        def bisect16(need):
            def bit_body(b, tb):
                cand_b = tb | lax.shift_left(jnp.int32(1), 15 - b)
                cnt = count_ge16((cand_b + I16_MIN).astype(I16))
                return jnp.where(cnt >= need, cand_b, tb)
            return lax.fori_loop(0, 16, bit_body, jnp.zeros((1, BLK), I32))

        t_hi = bisect16(n_sel) + I16_MIN
        above = count_ge16(jnp.minimum(t_hi + 1, -I16_MIN - 1).astype(I16))
        above = jnp.where(t_hi == -I16_MIN - 1, 0, above)

        def low_body(c, carry):
            rows = pl.ds(pl.multiple_of(c * (CHUNK * BLK), CHUNK * BLK), CHUNK * BLK)
            keys = key_ref[rows, :]
            low = (keys & 0xFFFF) + I16_MIN
            same = lax.shift_right_arithmetic(keys, 16) == t_hi
            half_ref[rows, :] = jnp.where(same, low, I16_MIN).astype(I16)
            return carry

        lax.fori_loop(0, n_chunks, low_body, 0)
        t_lo = bisect16(n_sel - above)
        thr = jnp.maximum(lax.shift_left(t_hi, 16) + t_lo, INT_MIN + 1)

        def count32(pred):
            def cbody(c, cnt):
                base = c * (CHUNK * BLK)
                keys = key_ref[pl.ds(pl.multiple_of(base, CHUNK * BLK), CHUNK * BLK), :]
                idx = base + lax.broadcasted_iota(I32, (CHUNK * BLK, BLK), 0)
                hit = jnp.where(pred(keys, idx), 1, 0).reshape(CHUNK * BLK // SUBLANES, SUBLANES, BLK)
                return cnt + jnp.sum(hit, axis=0)
            cnt = lax.fori_loop(0, n_chunks, cbody, jnp.zeros((SUBLANES, BLK), I32))
            return jnp.sum(cnt, axis=0, keepdims=True)

        excess = count32(lambda keys, idx: keys >= thr) - n_sel
        n_keys = n_chunks * (CHUNK * BLK)

        def tie_limit():
            def idx_body(c, carry):
                base = c * (CHUNK * BLK)
                rows = pl.ds(pl.multiple_of(base, CHUNK * BLK), CHUNK * BLK)
                idx = base + lax.broadcasted_iota(I32, (CHUNK * BLK, BLK), 0)
                half_ref[rows, :] = jnp.where(key_ref[rows, :] == thr, idx, I16_MAX).astype(I16)
                return carry

            lax.fori_loop(0, n_chunks, idx_body, 0)
            tied = n_keys - count_ge16(jnp.full((1, BLK), I16_MAX, I16))
            keep = tied - excess
            def bit_body(b, j):
                cand = j | lax.shift_left(jnp.int32(1), index_bits - 1 - b)
                below = n_keys - count_ge16(cand.astype(I16))
                return jnp.where(below <= keep - 1, cand, j)
            j = lax.fori_loop(0, index_bits, bit_body, jnp.zeros((1, BLK), I32))
            return jnp.where(excess > 0, j, n_keys)

        def mask_ties():
            last_tied = tie_limit()

            def body(c, carry):
                base = c * (CHUNK * BLK)
                rows = pl.ds(pl.multiple_of(base, CHUNK * BLK), CHUNK * BLK)
                keys = key_ref[rows, :]
                idx = base + lax.broadcasted_iota(I32, (CHUNK * BLK, BLK), 0)
                chosen = (keys > thr) | ((keys == thr) & (idx <= last_tied))
                key_ref[rows, :] = jnp.where(chosen, 0, NEG_BITS)
                return carry

            lax.fori_loop(0, n_chunks, body, 0)

        def mask_plain():
            def body(c, carry):
                rows = pl.ds(pl.multiple_of(c * (CHUNK * BLK), CHUNK * BLK), CHUNK * BLK)
                key_ref[rows, :] = jnp.where(key_ref[rows, :] >= thr, 0, NEG_BITS)
                return carry

            lax.fori_loop(0, n_chunks, body, 0)

        lax.cond(jnp.max(excess) > 0, mask_ties, mask_plain)

    slope2 = slopes_ref[h] * LOG2E
    q_ext = jnp.concatenate([q_ref[...], eq_ref[0]], axis=1)
    e_k = ek_ref[0]

    no_flags = [None] * CHUNK

    def chunk_meta(c):
        kbs = [jnp.minimum(c, n_chunks - 1) * CHUNK + b for b in range(CHUNK)]
        return kbs, [jnp.maximum(qt - kb, 0).astype(F32) * (-slope2 * BLK) for kb in kbs], no_flags, None

    def score_tile(kbs, b):
        rows = pl.ds(pl.multiple_of(kbs[b] * BLK, BLK), BLK)
        k_ext = jnp.concatenate([k_ref[rows, :], e_k], axis=1)
        return _scores_t(k_ext, q_ext) + pltpu.bitcast(key_ref[rows, :], F32)

    def value_tile(kbs, b):
        return vt_ref[0, kbs[b]]

    out_t = _attend(n_chunks, (chunk_meta, score_tile, value_tile, lambda c: c < n_chunks), slot_refs)
    o_ref[...] = jnp.transpose(out_t).astype(BF16)


def _dsa(z, v_t, slopes, e_k, e_q):
    s = z.shape[0]
    nb = s // BLK
    assert nb % CHUNK == 0 and s <= I16_MAX
    n_sel = min(DSA_TOPK_MAX, s // 4)
    grid_spec = pltpu.PrefetchScalarGridSpec(
        num_scalar_prefetch=1,
        grid=(nb, N_HEADS),
        in_specs=[pl.BlockSpec((BLK, IDX_HEADS * IDX_DIM), lambda i, h, sl: (i, OD_QI // (IDX_HEADS * IDX_DIM))),
                  pl.BlockSpec((BLK, 128), lambda i, h, sl: (i, OD_WI // 128)),
                  pl.BlockSpec((s, 128), lambda i, h, sl: (0, OD_KI // 128)),
                  pl.BlockSpec((BLK, HEAD_DIM), lambda i, h, sl: (i, h)),
                  pl.BlockSpec((s, HEAD_DIM), lambda i, h, sl: (0, N_HEADS + h)),
                  pl.BlockSpec((1, nb, V_ROWS, BLK), lambda i, h, sl: (h, 0, 0, 0)),
                  pl.BlockSpec((1, BLK, HEAD_DIM), lambda i, h, sl: (h, 0, 0)),
                  pl.BlockSpec((1, BLK, HEAD_DIM), lambda i, h, sl: (h, 0, 0))],
        out_specs=pl.BlockSpec((BLK, HEAD_DIM), lambda i, h, sl: (i, h)),
        scratch_shapes=[pltpu.VMEM((s, BLK), I32),
                        pltpu.VMEM((s, BLK), I16),
                        pltpu.VMEM((IDX_HEADS, BLK, 128), BF16)] + _pipeline_slots(),
    )
    return pl.pallas_call(
        functools.partial(_dsa_kernel, n_sel=n_sel),
        grid_spec=grid_spec,
        out_shape=jax.ShapeDtypeStruct((s, HALF), BF16),
        compiler_params=_cparams(2),
        name="dsa_attention",
    )(slopes, z, z, z, z, z, v_t, e_k, e_q)


def _v_transposed(z, col0):
    s = z.shape[0]
    nb = s // BLK
    v = z[:, col0:col0 + HALF].reshape(nb, BLK, N_HEADS, HEAD_DIM)
    v_t = jnp.transpose(v, (2, 0, 3, 1))
    ones = jnp.ones((N_HEADS, nb, 8, BLK), v_t.dtype)
    return jnp.concatenate([v_t, ones, jnp.zeros((N_HEADS, nb, V_ROWS - HEAD_DIM - 8, BLK), v_t.dtype)], axis=2)


def _qk_gain(qn, kn, col0, n):
    gq = jnp.tile(qn.astype(F32), N_HEADS) * (HEAD_DIM ** -0.5 * LOG2E)
    gk = jnp.tile(kn.astype(F32), N_HEADS)
    return jnp.concatenate([jnp.ones((col0,), F32), gq, gk, jnp.ones((n - col0 - 2 * HALF,), F32)])[None, :]


def kernel(x, c, norm_g, ada_w, ada_b, w_out, ev_w_in, pool_w, pool_scale, moba_qn, moba_kn,
           od_w_in, dsa_qn, dsa_kn, conv_w, conv_b, conv_ln_g, conv_ln_b):
    b, s, d = x.shape
    assert b == 1 and d == D_MODEL and s % BLK == 0
    depth = norm_g.shape[0]
    x2 = x[0].astype(F32)
    mods = _mods(c, ada_w, ada_b)[:, 0:1, :]
    slopes = jnp.exp2(-8.0 * jnp.arange(1, N_HEADS + 1, dtype=F32) / N_HEADS)
    e_k, e_q = _alibi_operands(slopes)

    for l in range(depth):
        shift, scl, gate_res = (mods[l, :, i * d:(i + 1) * d] for i in range(3))
        g = norm_g[l][None, :].astype(F32)
        if l % 2 == 0:
            e = l // 2
            w = ev_w_in[e].astype(BF16)
            z = _proj(x2, g, scl, shift, w, _qk_gain(moba_qn[e], moba_kn[e], HALF, EV_N), qk_col0=HALF)
            ya = _pool(z, 0, pool_w[e].astype(BF16), pool_scale[e][None, :].astype(F32))
            yb = _moba(z, N_HEADS, 2 * N_HEADS, _v_transposed(z, 3 * HALF), slopes, e_k, e_q)
            g_blk = 4
        else:
            o = l // 2
            w = od_w_in[o]
            c_qi = 3 * HALF
            c_wi = c_qi + IDX_HEADS * IDX_DIM
            c_ki = c_wi + IDX_HEADS
            c_ga = c_ki + IDX_DIM
            ki_w = w[:, c_ki:c_ga]
            w = jnp.concatenate([
                w[:, 0:3 * HALF], w[:, c_ga:],
                w[:, c_qi:c_wi], ki_w, ki_w,
                w[:, c_wi:c_ki], jnp.zeros((d, OD_N - OD_WI - IDX_HEADS), w.dtype)], axis=1).astype(BF16)
            z = _proj(x2, g, scl, shift, w, _qk_gain(dsa_qn[o], dsa_kn[o], 0, OD_N), qk_col0=0)
            ya = _dsa(z, _v_transposed(z, 2 * HALF), slopes, e_k, e_q)
            yb = _conv(z, 3, 4, conv_w[o].astype(F32), conv_b[o][None, :].astype(F32),
                       conv_ln_g[o][None, :].astype(F32), conv_ln_b[o][None, :].astype(F32))
            g_blk = 5
        x2 = _outproj(ya, yb, z, g_blk, x2, gate_res, w_out[l].astype(BF16))
    return x2[None].astype(x.dtype)
```
